```python
import jax, jax.numpy as jnp
from jax import lax
import numpy as np

D_MODEL = 1024
BATCH = 4
SEQ = 8192
DEPTH = 1

D_CONV = D_MODEL
CONV_WIDTH = 31
N_HEADS = 4
D_MLSTM = D_MODEL
HEAD_DIM = D_MLSTM // N_HEADS
QKV_BLOCK = 4
MLSTM_CONV_WIDTH = 4
CHUNK = 128
F_BIAS_LO = 3.0
F_BIAS_HI = 6.0
D_IN = 2 * D_CONV + 2 * D_MLSTM + 2 * D_MODEL
N_EXPERTS = 32
TOP_K = 4
D_EXPERT = D_MODEL
MOE_BLOCK = 128
SWIGLU_LIMIT = 7.0
SWIGLU_ALPHA = 1.702
EPS = 1e-6

kernel_name = 'hybrid_conformer_mlstm_moe_block'


def rms_norm(x, g):
    xf = x.astype(jnp.float32)
    y = xf * lax.rsqrt(jnp.mean(xf * xf, axis=-1, keepdims=True) + EPS)
    return (y * g.astype(jnp.float32)).astype(x.dtype)


def ada_norm(x, g, shift, scale):
    return rms_norm(x, g) * (1 + scale[:, None, :]) + shift[:, None, :]


def causal_dwconv(x, w, b):
    width = w.shape[0]
    y = lax.conv_general_dilated(
        x, w[:, None, :].astype(x.dtype), window_strides=(1,),
        padding=[(width - 1, 0)], dimension_numbers=('NWC', 'WIO', 'NWC'),
        feature_group_count=x.shape[-1])
    return y + b


def conformer_conv(u_glu, w_dw, b_dw, ln_g, ln_b, w_pw, b_pw):
    a, g = jnp.split(u_glu, 2, axis=-1)
    u = causal_dwconv(a * jax.nn.sigmoid(g), w_dw, b_dw)
    uf = u.astype(jnp.float32)
    mu = jnp.mean(uf, axis=-1, keepdims=True)
    var = jnp.mean(jnp.square(uf - mu), axis=-1, keepdims=True)
    u = ((uf - mu) * lax.rsqrt(var + EPS) * ln_g.astype(jnp.float32) + ln_b.astype(jnp.float32)).astype(u_glu.dtype)
    return jax.nn.silu(u) @ w_pw + b_pw


def blockdiag(x, w):
    xb = x.reshape(x.shape[:-1] + (w.shape[0], QKV_BLOCK))
    return jnp.einsum('bsni,nio->bsno', xb, w).reshape(x.shape)


def mlstm_chunkwise(q, k, v, ig, logf):
    B, S, H, dk = q.shape
    dv = v.shape[-1]
    nc = S // CHUNK

    def to_chunks(t):
        t = t.reshape((B, nc, CHUNK, H) + t.shape[3:])
        return jnp.moveaxis(jnp.moveaxis(t, 1, 0), 3, 2)

    tri = jnp.tril(jnp.ones((CHUNK, CHUNK), dtype=bool))

    def step(carry, inp):
        C, n, m = carry
        qc, kc, vc, ic, fc = inp
        b = jnp.cumsum(fc, axis=-1)
        d_log = jnp.where(tri, b[..., :, None] - b[..., None, :] + ic[..., None, :], -jnp.inf)
        inter = b + m[..., None]
        m_t = jnp.maximum(inter, jnp.max(d_log, axis=-1))
        w_intra = jnp.exp(d_log - m_t[..., None])
        w_inter = jnp.exp(inter - m_t)
        s = jnp.einsum('bhtd,bhsd->bhts', qc, kc) * w_intra
        num = jnp.einsum('bhts,bhsv->bhtv', s, vc) + w_inter[..., None] * jnp.einsum('bhtd,bhdv->bhtv', qc, C)
        den = jnp.sum(s, axis=-1) + w_inter * jnp.einsum('bhtd,bhd->bht', qc, n)
        h = num / jnp.maximum(jnp.abs(den), jnp.exp(-m_t))[..., None]
        b_last = b[..., -1]
        lw = b_last[..., None] - b + ic
        m_new = jnp.maximum(b_last + m, jnp.max(lw, axis=-1))
        decay = jnp.exp(b_last + m - m_new)
        wk = jnp.exp(lw - m_new[..., None])[..., None] * kc
        C_new = decay[..., None, None] * C + jnp.einsum('bhsd,bhsv->bhdv', wk, vc)
        n_new = decay[..., None] * n + jnp.sum(wk, axis=-2)
        return (C_new, n_new, m_new), h

    init = (jnp.zeros((B, H, dk, dv), jnp.float32), jnp.zeros((B, H, dk), jnp.float32),
            jnp.zeros((B, H), jnp.float32))
    _, hs = lax.scan(step, init, (to_chunks(q), to_chunks(k), to_chunks(v), to_chunks(ig), to_chunks(logf)))
    hs = jnp.moveaxis(jnp.moveaxis(hs, 2, 3), 0, 1)
    return hs.reshape(B, S, H * dv)


def mlstm_branch(x_m, o_pre, w_sc, b_sc, w_qbd, w_kbd, w_vbd, w_if, b_if, hn_g, skip, w_mo):
    B, S, _ = x_m.shape
    x_c = jax.nn.silu(causal_dwconv(x_m, w_sc, b_sc))
    q = blockdiag(x_c, w_qbd)
    k = blockdiag(x_c, w_kbd)
    v = blockdiag(x_m, w_vbd)
    gates = (q @ w_if[:D_MLSTM] + k @ w_if[D_MLSTM:2 * D_MLSTM] + v @ w_if[2 * D_MLSTM:] + b_if).astype(jnp.float32)
    ig = gates[..., :N_HEADS]
    logf = jax.nn.log_sigmoid(gates[..., N_HEADS:])
    heads = lambda t: t.reshape(B, S, N_HEADS, HEAD_DIM).astype(jnp.float32)
    h = mlstm_chunkwise(heads(q), heads(k) * (HEAD_DIM ** -0.5), heads(v), ig, logf)
    h = jax.nn.sigmoid(o_pre.astype(jnp.float32)) * h
    hh = h.reshape(B, S, N_HEADS, HEAD_DIM)
    mu = jnp.mean(hh, axis=-1, keepdims=True)
    var = jnp.mean(jnp.square(hh - mu), axis=-1, keepdims=True)
    hn = ((hh - mu) * lax.rsqrt(var + EPS)).reshape(B, S, D_MLSTM) * hn_g.astype(jnp.float32)
    out = hn.astype(x_m.dtype) + skip * x_c
    return out @ w_mo


def moe(h, w_r, b_r, w_gu, b_gu, w_dn, b_dn):
    B, S, D = h.shape
    T = B * S
    hf = h.reshape(T, D)
    logits = (hf @ w_r + b_r).astype(jnp.float32)
    top_v, top_i = lax.top_k(logits, TOP_K)
    top_w = jax.nn.softmax(top_v, axis=-1)
    A = T * TOP_K
    e_flat = top_i.reshape(A)
    tok_flat = jnp.arange(A, dtype=jnp.int32) // TOP_K
    w_flat = top_w.reshape(A)
    order = jnp.argsort(e_flat)
    e_sorted = e_flat[order]
    counts = jnp.bincount(e_flat, length=N_EXPERTS)
    padded = (counts + MOE_BLOCK - 1) // MOE_BLOCK * MOE_BLOCK
    start_u = jnp.cumsum(counts) - counts
    pad_end = jnp.cumsum(padded)
    start_p = pad_end - padded
    dest = start_p[e_sorted] + jnp.arange(A, dtype=jnp.int32) - start_u[e_sorted]
    P = A + N_EXPERTS * MOE_BLOCK
    NB = P // MOE_BLOCK
    row_tok = jnp.zeros((P,), jnp.int32).at[dest].set(tok_flat[order])
    row_w = jnp.zeros((P,), jnp.float32).at[dest].set(w_flat[order])
    blk_e = jnp.minimum(jnp.searchsorted(pad_end, jnp.arange(NB, dtype=jnp.int32) * MOE_BLOCK, side='right'),
                        N_EXPERTS - 1)

    def expert_block(args):
        e, toks, wts = args
        xb = hf[toks]
        gu = xb @ w_gu[e] + b_gu[e]
        gate, up = jnp.split(gu, 2, axis=-1)
        gate = jnp.minimum(gate, SWIGLU_LIMIT)
        up = jnp.clip(up, -SWIGLU_LIMIT, SWIGLU_LIMIT)
        glu = gate * jax.nn.sigmoid(gate * SWIGLU_ALPHA)
        y = ((up + 1) * glu) @ w_dn[e] + b_dn[e]
        return y * wts[:, None].astype(y.dtype)

    ys = lax.map(expert_block, (blk_e, row_tok.reshape(NB, MOE_BLOCK), row_w.reshape(NB, MOE_BLOCK)))
    out = jnp.zeros((T, D), h.dtype).at[row_tok].add(ys.reshape(P, D).astype(h.dtype))
    return out.reshape(B, S, D)


def setup_inputs(seed: int = 0) -> dict:
    key = jax.random.key(seed)
    ks = iter(jax.random.split(key, 40))
    nrm = lambda shape, scale: jax.random.normal(next(ks), shape, jnp.float32) * scale
    L = DEPTH
    f_bias = jnp.broadcast_to(jnp.linspace(F_BIAS_LO, F_BIAS_HI, N_HEADS, dtype=jnp.float32), (L, N_HEADS))
    nb = D_MLSTM // QKV_BLOCK
    return {
        'x': nrm((BATCH, SEQ, D_MODEL), 1.0),
        'c': nrm((BATCH, D_MODEL), 1.0),
        'w_ada': nrm((L, D_MODEL, 6 * D_MODEL), D_MODEL ** -0.5),
        'b_ada': nrm((L, 6 * D_MODEL), 0.01),
        'g_mix': 1.0 + nrm((L, D_MODEL), 0.02),
        'w_in': nrm((L, D_MODEL, D_IN), D_MODEL ** -0.5),
        'w_dw': nrm((L, CONV_WIDTH, D_CONV), CONV_WIDTH ** -0.5),
        'b_dw': nrm((L, D_CONV), 0.01),
        'ln_g': 1.0 + nrm((L, D_CONV), 0.02),
        'ln_b': nrm((L, D_CONV), 0.01),
        'w_conv_out': nrm((L, D_CONV, D_MODEL), D_CONV ** -0.5),
        'b_conv_out': nrm((L, D_MODEL), 0.01),
        'w_sc': nrm((L, MLSTM_CONV_WIDTH, D_MLSTM), MLSTM_CONV_WIDTH ** -0.5),
        'b_sc': nrm((L, D_MLSTM), 0.01),
        'w_qbd': nrm((L, nb, QKV_BLOCK, QKV_BLOCK), QKV_BLOCK ** -0.5),
        'w_kbd': nrm((L, nb, QKV_BLOCK, QKV_BLOCK), QKV_BLOCK ** -0.5),
        'w_vbd': nrm((L, nb, QKV_BLOCK, QKV_BLOCK), QKV_BLOCK ** -0.5),
        'w_if': nrm((L, 3 * D_MLSTM, 2 * N_HEADS), (3 * D_MLSTM) ** -0.5),
        'b_if': jnp.concatenate([nrm((L, N_HEADS), 0.1), f_bias + nrm((L, N_HEADS), 0.1)], axis=-1),
        'hn_g': 1.0 + nrm((L, D_MLSTM), 0.02),
        'skip': 1.0 + nrm((L, D_MLSTM), 0.02),
        'w_m_out': nrm((L, D_MLSTM, D_MODEL), D_MLSTM ** -0.5),
        'w_out': nrm((L, D_MODEL, D_MODEL), D_MODEL ** -0.5),
        'g_moe': 1.0 + nrm((L, D_MODEL), 0.02),
        'w_router': nrm((L, D_MODEL, N_EXPERTS), D_MODEL ** -0.5),
        'b_router': nrm((L, N_EXPERTS), 0.01),
        'w_gu': nrm((L, N_EXPERTS, D_MODEL, 2 * D_EXPERT), D_MODEL ** -0.5),
        'b_gu': nrm((L, N_EXPERTS, 2 * D_EXPERT), 0.01),
        'w_down': nrm((L, N_EXPERTS, D_EXPERT, D_MODEL), D_EXPERT ** -0.5),
        'b_down': nrm((L, N_EXPERTS, D_MODEL), 0.01),
        'g_final': 1.0 + nrm((D_MODEL,), 0.02),
    }


def reference(x, c, w_ada, b_ada, g_mix, w_in, w_dw, b_dw, ln_g, ln_b, w_conv_out, b_conv_out,
              w_sc, b_sc, w_qbd, w_kbd, w_vbd, w_if, b_if, hn_g, skip, w_m_out, w_out,
              g_moe, w_router, b_router, w_gu, b_gu, w_down, b_down, g_final):
    splits = [2 * D_CONV, 2 * D_CONV + D_MLSTM, 2 * D_CONV + 2 * D_MLSTM, 2 * D_CONV + 2 * D_MLSTM + D_MODEL]
    for l in range(DEPTH):
        mod = jax.nn.silu(c) @ w_ada[l] + b_ada[l]
        sh1, sc1, gt1, sh2, sc2, gt2 = jnp.split(mod, 6, axis=-1)
        h = ada_norm(x, g_mix[l], sh1, sc1)
        proj = h @ w_in[l]
        u_glu, x_m, o_pre, z_a, z_b = jnp.split(proj, splits, axis=-1)
        y_a = conformer_conv(u_glu, w_dw[l], b_dw[l], ln_g[l], ln_b[l], w_conv_out[l], b_conv_out[l])
        y_b = mlstm_branch(x_m, o_pre, w_sc[l], b_sc[l], w_qbd[l], w_kbd[l], w_vbd[l],
                           w_if[l], b_if[l], hn_g[l], skip[l], w_m_out[l])
        mixed = (jax.nn.sigmoid(z_a) * y_a + jax.nn.sigmoid(z_b) * y_b) @ w_out[l]
        x = x + gt1[:, None, :] * mixed
        hm = ada_norm(x, g_moe[l], sh2, sc2)
        x = x + gt2[:, None, :] * moe(hm, w_router[l], b_router[l], w_gu[l], b_gu[l], w_down[l], b_down[l])
    return rms_norm(x, g_final)
```

```python
import functools

import jax
import jax.numpy as jnp
from jax import lax
from jax.experimental import pallas as pl
from jax.experimental.pallas import tpu as pltpu

F32 = jnp.float32
BF16 = jnp.bfloat16
HIGHEST = lax.Precision.HIGHEST

D_MODEL = 1024
N_HEADS = 4
HEAD_DIM = 256
CHUNK = 128
CONV_WIDTH = 31
SC_WIDTH = 4
N_EXPERTS = 32
TOP_K = 4
SWIGLU_LIMIT = 7.0
SWIGLU_ALPHA = 1.702
EPS = 1e-6

LANES = 128
CONV_HALO = 32
SC_HALO = 16
VMEM_LIMIT = 56 * 1024 * 1024

TM_INPROJ = 512
TS_CONV = 512
TS_PRE = 512
TM_POST = 512
BM_EXPERT = 512
TR_ROWS = 256


def _cparams(sem, vmem=VMEM_LIMIT):
    return pltpu.CompilerParams(dimension_semantics=sem, vmem_limit_bytes=vmem)


def _sigmoid(v):
    return jax.nn.sigmoid(v)


def _ada_kernel(c_ref, w_ref, b_ref, o_ref):
    c = c_ref[...]
    s = c * _sigmoid(c)
    o_ref[...] = jnp.dot(s, w_ref[...], precision=HIGHEST, preferred_element_type=F32) + b_ref[...]


def _ada(c_pad, w_ada, b_ada):
    rows, d = c_pad.shape
    n = w_ada.shape[1]
    tn = 1024
    return pl.pallas_call(
        _ada_kernel,
        out_shape=jax.ShapeDtypeStruct((rows, n), F32),
        grid=(n // tn,),
        in_specs=[pl.BlockSpec((rows, d), lambda j: (0, 0)),
                  pl.BlockSpec((d, tn), lambda j: (0, j)),
                  pl.BlockSpec((1, tn), lambda j: (0, j))],
        out_specs=pl.BlockSpec((rows, tn), lambda j: (0, j)),
        compiler_params=_cparams(("arbitrary",)),
        name="ada",
    )(c_pad, w_ada, b_ada)


def _ada_norm(x, g, shift, scale):
    ms = jnp.mean(x * x, axis=-1, keepdims=True)
    y = x * lax.rsqrt(ms + EPS) * g
    return y * (1.0 + scale) + shift


def _inproj_kernel(x_ref, mod_ref, g_ref, w_ref, glu_ref, xm_ref, op_ref, za_ref, zb_ref):
    d = D_MODEL
    h = _ada_norm(x_ref[0], g_ref[...], mod_ref[0, 0:1, :], mod_ref[0, 1:2, :]).astype(BF16)

    def mm(k):
        return jnp.dot(h, w_ref[:, k * d:(k + 1) * d], preferred_element_type=F32)

    a = mm(0)
    glu_ref[0] = (a * _sigmoid(mm(1))).astype(BF16)
    xm_ref[0] = mm(2).astype(BF16)
    op_ref[0] = mm(3).astype(BF16)
    za_ref[0] = mm(4).astype(BF16)
    zb_ref[0] = mm(5).astype(BF16)


def _inproj(x, mod, g_mix, w_in_bf):
    b, s, d = x.shape
    tm = min(TM_INPROJ, s)
    tok = pl.BlockSpec((1, tm, d), lambda bi, i: (bi, i, 0))
    out = jax.ShapeDtypeStruct((b, s, d), BF16)
    return pl.pallas_call(
        _inproj_kernel,
        out_shape=[out] * 5,
        grid=(b, s // tm),
        in_specs=[tok,
                  pl.BlockSpec((1, 6, d), lambda bi, i: (bi, 0, 0)),
                  pl.BlockSpec((1, d), lambda bi, i: (0, 0)),
                  pl.BlockSpec(w_in_bf.shape, lambda bi, i: (0, 0))],
        out_specs=[tok] * 5,
        compiler_params=_cparams(("arbitrary", "arbitrary")),
        name="inproj",
    )(x, mod, g_mix, w_in_bf)


def _conv_kernel(u_ref, halo_ref, wdw_ref, bdw_ref, lng_ref, lnb_ref, o_ref, ubuf, ybuf):
    i = pl.program_id(1)
    ts = u_ref.shape[1]
    halo = halo_ref[0].astype(F32)
    ubuf[0:CONV_HALO, :] = jnp.where(i > 0, halo, 0.0)
    ubuf[CONV_HALO:CONV_HALO + ts, :] = u_ref[0].astype(F32)
    rb = 64
    first = CONV_HALO - (CONV_WIDTH - 1)

    def lane_tile(c, carry):
        cols = pl.ds(pl.multiple_of(c * LANES, LANES), LANES)
        bias = bdw_ref[:, cols]
        for r in range(ts // rb):
            acc = jnp.zeros((rb, LANES), F32) + bias
            for j in range(CONV_WIDTH):
                acc = acc + wdw_ref[j:j + 1, cols] * ubuf[r * rb + first + j:r * rb + first + j + rb, cols]
            ybuf[r * rb:(r + 1) * rb, cols] = acc
        return carry

    lax.fori_loop(0, D_MODEL // LANES, lane_tile, 0)
    y = ybuf[...]
    mu = jnp.mean(y, axis=-1, keepdims=True)
    yc = y - mu
    var = jnp.mean(yc * yc, axis=-1, keepdims=True)
    u = yc * lax.rsqrt(var + EPS) * lng_ref[...] + lnb_ref[...]
    o_ref[0] = (u * _sigmoid(u)).astype(BF16)


def _conv_branch(glu, w_dw, b_dw, ln_g, ln_b):
    b, s, d = glu.shape
    ts = min(TS_CONV, s)
    hb = ts // CONV_HALO
    tok = pl.BlockSpec((1, ts, d), lambda bi, i: (bi, i, 0))
    vec = pl.BlockSpec((1, d), lambda bi, i: (0, 0))
    return pl.pallas_call(
        _conv_kernel,
        out_shape=jax.ShapeDtypeStruct((b, s, d), BF16),
        grid=(b, s // ts),
        in_specs=[tok,
                  pl.BlockSpec((1, CONV_HALO, d), lambda bi, i: (bi, jnp.maximum(i * hb - 1, 0), 0)),
                  pl.BlockSpec((CONV_WIDTH, d), lambda bi, i: (0, 0)),
                  vec, vec, vec],
        out_specs=tok,
        scratch_shapes=[pltpu.VMEM((ts + CONV_HALO, d), F32), pltpu.VMEM((ts, d), F32)],
        compiler_params=_cparams(("arbitrary", "arbitrary")),
        name="conv",
    )(glu, glu, w_dw, b_dw, ln_g, ln_b)


def _mlstm_pre_kernel(xm_ref, halo_ref, wsc_ref, bsc_ref, bdq_ref, bdk_ref, bdkt_ref, bdv_ref,
                      wq_ref, wk_ref, wv_ref, bif_ref,
                      xc_ref, q_ref, kt_ref, v_ref, bc_ref, g_ref, xbuf):
    i = pl.program_id(1)
    ts = xm_ref.shape[1]
    xm_bf = xm_ref[0]
    xbuf[0:SC_HALO, :] = jnp.where(i > 0, halo_ref[0].astype(F32), 0.0)
    xbuf[SC_HALO:SC_HALO + ts, :] = xm_bf.astype(F32)
    first = SC_HALO - (SC_WIDTH - 1)
    y = jnp.zeros((ts, D_MODEL), F32) + bsc_ref[...]
    for j in range(SC_WIDTH):
        y = y + wsc_ref[j:j + 1, :] * xbuf[first + j:first + j + ts, :]
    xc = y * _sigmoid(y)
    xc_bf = xc.astype(BF16)
    xc_ref[0] = xc_bf

    gates = jnp.zeros((ts, 2 * LANES), F32) + bif_ref[...]
    for c in range(D_MODEL // LANES):
        cols = slice(c * LANES, (c + 1) * LANES)
        xcc = xc_bf[:, cols]
        qc = jnp.dot(xcc, bdq_ref[c], preferred_element_type=F32).astype(BF16)
        kc = jnp.dot(xcc, bdk_ref[c], preferred_element_type=F32).astype(BF16)
        vc = jnp.dot(xm_bf[:, cols], bdv_ref[c], preferred_element_type=F32).astype(BF16)
        ktc = lax.dot_general(bdkt_ref[c], xcc, (((1,), (1,)), ((), ())), preferred_element_type=F32)
        q_ref[0, :, cols] = qc
        v_ref[0, :, cols] = vc
        kt_ref[0, cols, :] = (ktc * (HEAD_DIM ** -0.5)).astype(BF16)
        gates = gates + jnp.dot(qc, wq_ref[cols, :], preferred_element_type=F32)
        gates = gates + jnp.dot(kc, wk_ref[cols, :], preferred_element_type=F32)
        gates = gates + jnp.dot(vc, wv_ref[cols, :], preferred_element_type=F32)

    ig = gates[:, :LANES]
    fpre = gates[:, LANES:]
    lane = lax.broadcasted_iota(jnp.int32, (ts, LANES), 1)
    logf = jnp.minimum(fpre, 0.0) - jnp.log(1.0 + jnp.exp(-jnp.abs(fpre)))
    logf = jnp.where(lane < N_HEADS, logf, 0.0)
    row = lax.broadcasted_iota(jnp.int32, (CHUNK, CHUNK), 0)
    col = lax.broadcasted_iota(jnp.int32, (CHUNK, CHUNK), 1)
    tri = (col <= row).astype(F32)
    for ch in range(ts // CHUNK):
        rows = slice(ch * CHUNK, (ch + 1) * CHUNK)
        bcum = jnp.dot(tri, logf[rows, :], precision=HIGHEST, preferred_element_type=F32)
        bc_ref[0, rows, :] = bcum
        gt = (ig[rows, :] - bcum).T
        g_ref[0, ch] = gt[0:8, :]


def _mlstm_pre(xm, w_sc, b_sc, bdq, bdk, bdkt, bdv, wq, wk, wv, bif):
    b, s, d = xm.shape
    ts = min(TS_PRE, s)
    hb = ts // SC_HALO
    nch = ts // CHUNK
    tok = pl.BlockSpec((1, ts, d), lambda bi, i: (bi, i, 0))
    vec = pl.BlockSpec((1, d), lambda bi, i: (0, 0))
    bd = pl.BlockSpec((d // LANES, LANES, LANES), lambda bi, i: (0, 0, 0))
    wg = pl.BlockSpec((d, 2 * LANES), lambda bi, i: (0, 0))
    tok_bf = jax.ShapeDtypeStruct((b, s, d), BF16)
    return pl.pallas_call(
        _mlstm_pre_kernel,
        out_shape=[tok_bf, tok_bf,
                   jax.ShapeDtypeStruct((b, d, s), BF16),
                   tok_bf,
                   jax.ShapeDtypeStruct((b, s, LANES), F32),
                   jax.ShapeDtypeStruct((b, s // CHUNK, 8, CHUNK), F32)],
        grid=(b, s // ts),
        in_specs=[tok,
                  pl.BlockSpec((1, SC_HALO, d), lambda bi, i: (bi, jnp.maximum(i * hb - 1, 0), 0)),
                  pl.BlockSpec((SC_WIDTH, d), lambda bi, i: (0, 0)),
                  vec, bd, bd, bd, bd, wg, wg, wg,
                  pl.BlockSpec((1, 2 * LANES), lambda bi, i: (0, 0))],
        out_specs=[tok, tok,
                   pl.BlockSpec((1, d, ts), lambda bi, i: (bi, 0, i)),
                   tok,
                   pl.BlockSpec((1, ts, LANES), lambda bi, i: (bi, i, 0)),
                   pl.BlockSpec((1, nch, 8, CHUNK), lambda bi, i: (bi, i, 0, 0))],
        scratch_shapes=[pltpu.VMEM((ts + SC_HALO, d), F32)],
        compiler_params=_cparams(("arbitrary", "arbitrary")),
        name="mlstm_pre",
    )(xm, xm, w_sc, b_sc, bdq, bdk, bdkt, bdv, wq, wk, wv, bif)


def _mlstm_kernel(q_ref, kt_ref, v_ref, bc_ref, g_ref, op_ref, xc_ref, hng_ref, skip_ref,
                  o_ref, c_scr, n_scr, m_scr):
    ci = pl.program_id(1)

    @pl.when(ci == 0)
    def _():
        c_scr[...] = jnp.zeros_like(c_scr)
        n_scr[...] = jnp.zeros_like(n_scr)
        m_scr[...] = jnp.zeros_like(m_scr)

    row = lax.broadcasted_iota(jnp.int32, (CHUNK, CHUNK), 0)
    col = lax.broadcasted_iota(jnp.int32, (CHUNK, CHUNK), 1)
    causal = col <= row
    for h in range(N_HEADS):
        hs = slice(h * HEAD_DIM, (h + 1) * HEAD_DIM)
        qh = q_ref[0, :, hs]
        kth = kt_ref[0, hs, :]
        vh = v_ref[0, :, hs]
        bcol = bc_ref[0, :, h:h + 1]
        grow = g_ref[0, 0, h:h + 1, :]
        m_prev = m_scr[h][:, 0:1]
        c_prev = c_scr[h]
        n_prev = n_scr[h]

        dlog = jnp.where(causal, bcol + grow, -jnp.inf)
        inter = bcol + m_prev
        m_t = jnp.maximum(inter, jnp.max(dlog, axis=-1, keepdims=True))
        w_intra = jnp.exp(dlog - m_t)
        w_inter = jnp.exp(inter - m_t)
        sc = jnp.dot(qh, kth, preferred_element_type=F32) * w_intra
        q_c = jnp.dot(qh, c_prev.astype(BF16), preferred_element_type=F32)
        q_n = jnp.dot(qh, n_prev.astype(BF16), preferred_element_type=F32)
        num = jnp.dot(sc.astype(BF16), vh, preferred_element_type=F32) + w_inter * q_c
        den = jnp.sum(sc, axis=-1, keepdims=True) + w_inter * q_n
        den = jnp.maximum(jnp.abs(den), jnp.exp(-m_t))
        hh = num / jnp.concatenate([den, den], axis=-1)

        hg = _sigmoid(op_ref[0, :, hs].astype(F32)) * hh
        mu = jnp.mean(hg, axis=-1, keepdims=True)
        hc = hg - mu
        var = jnp.mean(hc * hc, axis=-1, keepdims=True)
        hn = hc * lax.rsqrt(var + EPS) * hng_ref[:, hs]
        o_ref[0, :, hs] = (hn + skip_ref[:, hs] * xc_ref[0, :, hs].astype(F32)).astype(BF16)

        b_last = bcol[CHUNK - 1:CHUNK, :]
        m_new = jnp.maximum(b_last + m_prev, b_last + jnp.max(grow, axis=-1, keepdims=True))
        decay = jnp.exp(b_last + m_prev - m_new)
        wrow = jnp.exp(b_last + grow - m_new)
        wkt = kth.astype(F32) * wrow
        c_scr[h] = decay * c_prev + jnp.dot(wkt.astype(BF16), vh, preferred_element_type=F32)
        n_scr[h] = decay * n_prev + jnp.sum(wkt, axis=-1, keepdims=True)
        m_scr[h] = jnp.broadcast_to(m_new, (1, LANES))


def _mlstm(q, kt, v, bc, g, o_pre, xc, hn_g, skip):
    b, s, d = q.shape
    nc = s // CHUNK
    tok = pl.BlockSpec((1, CHUNK, d), lambda bi, ci: (bi, ci, 0))
    vec = pl.BlockSpec((1, d), lambda bi, ci: (0, 0))
    return pl.pallas_call(
        _mlstm_kernel,
        out_shape=jax.ShapeDtypeStruct((b, s, d), BF16),
        grid=(b, nc),
        in_specs=[tok,
                  pl.BlockSpec((1, d, CHUNK), lambda bi, ci: (bi, 0, ci)),
                  tok,
                  pl.BlockSpec((1, CHUNK, LANES), lambda bi, ci: (bi, ci, 0)),
                  pl.BlockSpec((1, 1, 8, CHUNK), lambda bi, ci: (bi, ci, 0, 0)),
                  tok, tok, vec, vec],
        out_specs=tok,
        scratch_shapes=[pltpu.VMEM((N_HEADS, HEAD_DIM, HEAD_DIM), F32),
                        pltpu.VMEM((N_HEADS, HEAD_DIM, LANES), F32),
                        pltpu.VMEM((N_HEADS, 1, LANES), F32)],
        compiler_params=_cparams(("arbitrary", "arbitrary")),
        name="mlstm",
    )(q, kt, v, bc, g, o_pre, xc, hn_g, skip)


def _post_kernel(x_ref, aa_ref, mo_ref, za_ref, zb_ref, mod_ref, wpw_ref, bpw_ref, wmo_ref, wout_ref,
                 gmoe_ref, wr_ref, br_ref,
                 x1_ref, hm_ref, mi_ref, mw_ref, cnt_ref, carry):
    first = (pl.program_id(0) == 0) & (pl.program_id(1) == 0)

    @pl.when(first)
    def _():
        carry[...] = jnp.zeros_like(carry)

    tm = x_ref.shape[1]
    y_a = jnp.dot(aa_ref[0], wpw_ref[...], preferred_element_type=F32) + bpw_ref[...]
    y_b = jnp.dot(mo_ref[0], wmo_ref[...], preferred_element_type=F32)
    merged = _sigmoid(za_ref[0].astype(F32)) * y_a + _sigmoid(zb_ref[0].astype(F32)) * y_b
    mixed = jnp.dot(merged.astype(BF16), wout_ref[...], preferred_element_type=F32)
    x1 = x_ref[0] + mod_ref[0, 2:3, :] * mixed
    x1_ref[0] = x1
    hm = _ada_norm(x1, gmoe_ref[...], mod_ref[0, 3:4, :], mod_ref[0, 4:5, :])
    hm_ref[0] = hm

    logits = jnp.dot(hm, wr_ref[...], precision=HIGHEST, preferred_element_type=F32) + br_ref[...]
    lane = lax.broadcasted_iota(jnp.int32, (tm, LANES), 1)
    work = jnp.where(lane < N_EXPERTS, logits, -jnp.inf)
    vals, idxs = [], []
    for _ in range(TOP_K):
        mx = jnp.max(work, axis=-1, keepdims=True)
        ix = jnp.min(jnp.where(work == mx, lane, LANES), axis=-1, keepdims=True)
        vals.append(mx)
        idxs.append(ix)
        work = jnp.where(lane == ix, -jnp.inf, work)
    exps = [jnp.exp(v - vals[0]) for v in vals]
    tot = exps[0] + exps[1] + exps[2] + exps[3]

    onehots = [lane == ix for ix in idxs]
    occ = jnp.zeros((tm, LANES), F32)
    for oh in onehots:
        occ = occ + oh.astype(F32)
    r_i = lax.broadcasted_iota(jnp.int32, (tm, tm), 0)
    c_i = lax.broadcasted_iota(jnp.int32, (tm, tm), 1)
    strict = (c_i < r_i).astype(BF16)
    before = jnp.dot(strict, occ.astype(BF16), preferred_element_type=F32) + carry[...]
    mi = jnp.zeros((tm, LANES), jnp.int32)
    mw = jnp.zeros((tm, LANES), F32)
    for j in range(TOP_K):
        rank = jnp.sum(jnp.where(onehots[j], before, 0.0), axis=-1, keepdims=True).astype(jnp.int32)
        mi = jnp.where(lane == j, idxs[j], mi)
        mi = jnp.where(lane == TOP_K + j, rank, mi)
        mw = jnp.where(lane == j, exps[j] / tot, mw)
    mi_ref[0] = mi
    mw_ref[0] = mw
    total = carry[...] + jnp.sum(occ, axis=0, keepdims=True)
    carry[...] = total
    cnt_ref[...] = jnp.broadcast_to(total, cnt_ref.shape)


def _post(x, act_a, mo_in, z_a, z_b, mod, w_pw, b_pw, w_mo, w_out, g_moe, w_r, b_r):
    b, s, d = x.shape
    tm = min(TM_POST, s)
    tok = pl.BlockSpec((1, tm, d), lambda bi, i: (bi, i, 0))
    meta = pl.BlockSpec((1, tm, LANES), lambda bi, i: (bi, i, 0))
    vec = pl.BlockSpec((1, d), lambda bi, i: (0, 0))
    mat = pl.BlockSpec((d, d), lambda bi, i: (0, 0))
    return pl.pallas_call(
        _post_kernel,
        out_shape=[jax.ShapeDtypeStruct((b, s, d), F32),
                   jax.ShapeDtypeStruct((b, s, d), F32),
                   jax.ShapeDtypeStruct((b, s, LANES), jnp.int32),
                   jax.ShapeDtypeStruct((b, s, LANES), F32),
                   jax.ShapeDtypeStruct((8, LANES), F32)],
        grid=(b, s // tm),
        in_specs=[tok, tok, tok, tok, tok,
                  pl.BlockSpec((1, 6, d), lambda bi, i: (bi, 0, 0)),
                  mat, vec, mat, mat, vec,
                  pl.BlockSpec((d, LANES), lambda bi, i: (0, 0)),
                  pl.BlockSpec((1, LANES), lambda bi, i: (0, 0))],
        out_specs=[tok, tok, meta, meta, pl.BlockSpec((8, LANES), lambda bi, i: (0, 0))],
        scratch_shapes=[pltpu.VMEM((1, LANES), F32)],
        compiler_params=_cparams(("arbitrary", "arbitrary")),
        name="post",
    )(x, act_a, mo_in, z_a, z_b, mod, w_pw, b_pw, w_mo, w_out, g_moe, w_r, b_r)


def _dispatch_kernel(zlist_ref, dest_ref, hm_hbm, xs_hbm, zbuf, sem, zsem):
    i = pl.program_id(0)
    n = pl.num_programs(0)
    t0 = i * TR_ROWS

    @pl.when(i == 0)
    def _():
        zbuf[...] = jnp.zeros_like(zbuf)

        def zero_copy(blk):
            return pltpu.make_async_copy(zbuf, xs_hbm.at[pl.ds(blk * BM_EXPERT, BM_EXPERT)], zsem)

        def start(k, carry):
            @pl.when(zlist_ref[k] >= 0)
            def _():
                zero_copy(zlist_ref[k]).start()
            return carry

        def wait(k, carry):
            @pl.when(zlist_ref[k] >= 0)
            def _():
                zero_copy(zlist_ref[k]).wait()
            return carry

        lax.fori_loop(0, zlist_ref.shape[0], start, 0)
        lax.fori_loop(0, zlist_ref.shape[0], wait, 0)

    def body(r, carry):
        for j in range(TOP_K):
            dst = dest_ref[TOP_K * r + j]
            pltpu.make_async_copy(hm_hbm.at[pl.ds(t0 + r, 1)], xs_hbm.at[pl.ds(dst, 1)], sem).start()
        return carry

    lax.fori_loop(0, TR_ROWS, body, 0)

    def wait_step():
        pltpu.make_async_copy(hm_hbm.at[pl.ds(0, TOP_K * TR_ROWS)],
                              xs_hbm.at[pl.ds(0, TOP_K * TR_ROWS)], sem).wait()

    @pl.when(i > 0)
    def _():
        wait_step()

    @pl.when(i == n - 1)
    def _():
        wait_step()


def _dispatch(zlist, dest_flat, hm_flat, n_rows):
    t, d = hm_flat.shape
    blk = TOP_K * TR_ROWS
    grid_spec = pltpu.PrefetchScalarGridSpec(
        num_scalar_prefetch=1,
        grid=(t // TR_ROWS,),
        in_specs=[pl.BlockSpec((blk,), lambda i, zl: (i,), memory_space=pltpu.SMEM),
                  pl.BlockSpec(memory_space=pl.ANY)],
        out_specs=pl.BlockSpec(memory_space=pl.ANY),
        scratch_shapes=[pltpu.VMEM((BM_EXPERT, d), F32),
                        pltpu.SemaphoreType.DMA(()), pltpu.SemaphoreType.DMA(())],
    )
    return pl.pallas_call(
        _dispatch_kernel,
        out_shape=jax.ShapeDtypeStruct((n_rows, d), F32),
        grid_spec=grid_spec,
        compiler_params=_cparams(("arbitrary",)),
        name="dispatch",
    )(zlist, dest_flat, hm_flat)


def _expert_kernel(be_ref, nu_ref, xs_ref, wgu_ref, bgu_ref, wdn_ref, bdn_ref, ys_ref, wgu_bf, wdn_bf):
    i = pl.program_id(0)
    d = D_MODEL
    prev = be_ref[jnp.maximum(i - 1, 0)]
    changed = (i == 0) | (be_ref[i] != prev)

    @pl.when(changed)
    def _():
        wgu_bf[...] = wgu_ref[0].astype(BF16)
        wdn_bf[...] = wdn_ref[0].astype(BF16)

    @pl.when(i < nu_ref[0])
    def _():
        x = xs_ref[...].astype(BF16)
        acc = jnp.zeros(ys_ref.shape, F32) + bdn_ref[0]
        ck = 512
        for n in range(d // ck):
            cg = slice(n * ck, (n + 1) * ck)
            cu = slice(d + n * ck, d + (n + 1) * ck)
            gate = jnp.dot(x, wgu_bf[:, cg], preferred_element_type=F32) + bgu_ref[0, :, cg]
            up = jnp.dot(x, wgu_bf[:, cu], preferred_element_type=F32) + bgu_ref[0, :, cu]
            gate = jnp.minimum(gate, SWIGLU_LIMIT)
            up = jnp.clip(up, -SWIGLU_LIMIT, SWIGLU_LIMIT)
            glu = gate * _sigmoid(gate * SWIGLU_ALPHA)
            act = ((up + 1.0) * glu).astype(BF16)
            acc = acc + jnp.dot(act, wdn_bf[cg, :], preferred_element_type=F32)
        ys_ref[...] = acc

    @pl.when(i >= nu_ref[0])
    def _():
        ys_ref[...] = jnp.zeros_like(ys_ref)


def _experts(blk_e, n_used, xs, w_gu, b_gu, w_dn, b_dn):
    p, d = xs.shape
    bm = BM_EXPERT
    e = w_gu.shape[0]
    grid_spec = pltpu.PrefetchScalarGridSpec(
        num_scalar_prefetch=2,
        grid=(p // bm,),
        in_specs=[pl.BlockSpec((bm, d), lambda i, be, nu: (i, 0)),
                  pl.BlockSpec((1, d, 2 * d), lambda i, be, nu: (be[i], 0, 0)),
                  pl.BlockSpec((1, 1, 2 * d), lambda i, be, nu: (be[i], 0, 0)),
                  pl.BlockSpec((1, d, d), lambda i, be, nu: (be[i], 0, 0)),
                  pl.BlockSpec((1, 1, d), lambda i, be, nu: (be[i], 0, 0))],
        out_specs=pl.BlockSpec((bm, d), lambda i, be, nu: (i, 0)),
        scratch_shapes=[pltpu.VMEM((d, 2 * d), BF16), pltpu.VMEM((d, d), BF16)],
    )
    return pl.pallas_call(
        _expert_kernel,
        out_shape=jax.ShapeDtypeStruct((p, d), F32),
        grid_spec=grid_spec,
        compiler_params=_cparams(("arbitrary",)),
        name="experts",
    )(blk_e, n_used, xs, w_gu, b_gu.reshape(e, 1, 2 * d), w_dn, b_dn.reshape(e, 1, d))


def _combine_kernel(dcur_ref, dnext_ref, ys_hbm, x1_ref, mw_ref, mod_ref, gf_ref, o_ref, gbuf, sem):
    i = pl.program_id(0)
    n = pl.num_programs(0)
    slot = i % 2

    def issue(dref, sl):
        def body(r, carry):
            for j in range(TOP_K):
                src = dref[TOP_K * r + j]
                pltpu.make_async_copy(ys_hbm.at[pl.ds(src, 1)], gbuf.at[sl, j, pl.ds(r, 1)],
                                      sem.at[sl]).start()
            return carry
        lax.fori_loop(0, TR_ROWS, body, 0)

    @pl.when(i == 0)
    def _():
        issue(dcur_ref, 0)

    @pl.when(i + 1 < n)
    def _():
        issue(dnext_ref, 1 - slot)

    for j in range(TOP_K):
        pltpu.make_async_copy(ys_hbm.at[pl.ds(0, TR_ROWS)], gbuf.at[slot, j], sem.at[slot]).wait()

    moe = jnp.zeros((TR_ROWS, D_MODEL), F32)
    for j in range(TOP_K):
        moe = moe + mw_ref[:, j:j + 1] * gbuf[slot, j]
    xo = x1_ref[...] + mod_ref[0, 5:6, :] * moe
    ms = jnp.mean(xo * xo, axis=-1, keepdims=True)
    o_ref[...] = xo * lax.rsqrt(ms + EPS) * gf_ref[...]


def _combine(dest_flat, ys, x1_flat, mw_flat, mod, g_final, seq):
    t, d = x1_flat.shape
    blk = TOP_K * TR_ROWS
    nsteps = t // TR_ROWS
    per_seq = seq // TR_ROWS
    return pl.pallas_call(
        _combine_kernel,
        out_shape=jax.ShapeDtypeStruct((t, d), F32),
        grid=(nsteps,),
        in_specs=[pl.BlockSpec((blk,), lambda i: (i,), memory_space=pltpu.SMEM),
                  pl.BlockSpec((blk,), lambda i: (jnp.minimum(i + 1, nsteps - 1),), memory_space=pltpu.SMEM),
                  pl.BlockSpec(memory_space=pl.ANY),
                  pl.BlockSpec((TR_ROWS, d), lambda i: (i, 0)),
                  pl.BlockSpec((TR_ROWS, LANES), lambda i: (i, 0)),
                  pl.BlockSpec((1, 6, d), lambda i: (i // per_seq, 0, 0)),
                  pl.BlockSpec((1, d), lambda i: (0, 0))],
        out_specs=pl.BlockSpec((TR_ROWS, d), lambda i: (i, 0)),
        scratch_shapes=[pltpu.VMEM((2, TOP_K, TR_ROWS, d), F32), pltpu.SemaphoreType.DMA((2,))],
        compiler_params=_cparams(("arbitrary",)),
        name="combine",
    )(dest_flat, dest_flat, ys, x1_flat, mw_flat, mod, g_final)


def _blockdiag_tiles(w):
    nb = w.shape[0]
    per = LANES // 4
    wt = w.reshape(nb // per, per, 4, 4)
    eye = jnp.eye(per, dtype=w.dtype)
    return jnp.einsum('cnio,nm->cnimo', wt, eye).reshape(nb // per, LANES, LANES)


def _gate_weights(w_if_part):
    out = jnp.zeros((w_if_part.shape[0], 2 * LANES), F32)
    out = out.at[:, :N_HEADS].set(w_if_part[:, :N_HEADS])
    out = out.at[:, LANES:LANES + N_HEADS].set(w_if_part[:, N_HEADS:])
    return out.astype(BF16)


def _layer(x, mod, p):
    b, s, d = x.shape
    t = b * s
    row = lambda v: v.reshape(1, -1)

    glu, xm, o_pre, z_a, z_b = _inproj(x, mod, row(p['g_mix']), p['w_in'].astype(BF16))
    act_a = _conv_branch(glu, p['w_dw'], row(p['b_dw']), row(p['ln_g']), row(p['ln_b']))

    bdq = _blockdiag_tiles(p['w_qbd']).astype(BF16)
    bdk = _blockdiag_tiles(p['w_kbd']).astype(BF16)
    bdv = _blockdiag_tiles(p['w_vbd']).astype(BF16)
    bdkt = jnp.swapaxes(bdk, 1, 2)
    w_if = p['w_if']
    bif = jnp.zeros((1, 2 * LANES), F32)
    bif = bif.at[0, :N_HEADS].set(p['b_if'][:N_HEADS]).at[0, LANES:LANES + N_HEADS].set(p['b_if'][N_HEADS:])
    xc, q, kt, v, bc, g = _mlstm_pre(xm, p['w_sc'], row(p['b_sc']), bdq, bdk, bdkt, bdv,
                                     _gate_weights(w_if[:d]), _gate_weights(w_if[d:2 * d]),
                                     _gate_weights(w_if[2 * d:]), bif)
    mo_in = _mlstm(q, kt, v, bc, g, o_pre, xc, row(p['hn_g']), row(p['skip']))

    w_r = jnp.zeros((d, LANES), F32).at[:, :N_EXPERTS].set(p['w_router'])
    b_r = jnp.zeros((1, LANES), F32).at[0, :N_EXPERTS].set(p['b_router'])
    x1, hm, mi, mw, cnt = _post(x, act_a, mo_in, z_a, z_b, mod,
                                p['w_conv_out'].astype(BF16), row(p['b_conv_out']),
                                p['w_m_out'].astype(BF16), p['w_out'].astype(BF16),
                                row(p['g_moe']), w_r, b_r)

    bm = BM_EXPERT
    counts = cnt[0, :N_EXPERTS].astype(jnp.int32)
    padded = (counts + bm - 1) // bm * bm
    pad_end = jnp.cumsum(padded)
    start_p = pad_end - padded
    n_rows = t * TOP_K + N_EXPERTS * bm
    nblk = n_rows // bm
    blk_e = jnp.minimum(jnp.searchsorted(pad_end, jnp.arange(nblk, dtype=jnp.int32) * bm, side='right'),
                        N_EXPERTS - 1).astype(jnp.int32)
    n_used = (pad_end[-1:] // bm).astype(jnp.int32)
    mi = mi.reshape(t, LANES)
    dest = (start_p[mi[:, :TOP_K]] + mi[:, TOP_K:2 * TOP_K]).reshape(t * TOP_K)

    last_blk = jnp.where(padded > 0, pad_end // bm - 1, -1)
    tail_blk = n_used[0] + jnp.arange(N_EXPERTS, dtype=jnp.int32)
    zlist = jnp.concatenate([last_blk, jnp.where(tail_blk < nblk, tail_blk, -1)]).astype(jnp.int32)

    xs = _dispatch(zlist, dest, hm.reshape(t, d), n_rows)
    ys = _experts(blk_e, n_used, xs, p['w_gu'], p['b_gu'], p['w_down'], p['b_down'])
    out = _combine(dest, ys, x1.reshape(t, d), mw.reshape(t, LANES), mod, p['g_final'], s)
    return out.reshape(b, s, d)


def kernel(x, c, w_ada, b_ada, g_mix, w_in, w_dw, b_dw, ln_g, ln_b, w_conv_out, b_conv_out, w_sc, b_sc,
           w_qbd, w_kbd, w_vbd, w_if, b_if, hn_g, skip, w_m_out, w_out, g_moe, w_router, b_router,
           w_gu, b_gu, w_down, b_down, g_final):
    b, s, d = x.shape
    depth = w_ada.shape[0]
    assert depth == 1 and d == D_MODEL and s % 512 == 0
    rows = -(-b // 8) * 8
    c_pad = jnp.zeros((rows, d), F32).at[:b].set(c)
    l = 0
    mod = _ada(c_pad, w_ada[l], b_ada[l].reshape(1, -1))[:b].reshape(b, 6, d)
    p = dict(g_mix=g_mix[l], w_in=w_in[l], w_dw=w_dw[l], b_dw=b_dw[l], ln_g=ln_g[l], ln_b=ln_b[l],
             w_conv_out=w_conv_out[l], b_conv_out=b_conv_out[l], w_sc=w_sc[l], b_sc=b_sc[l],
             w_qbd=w_qbd[l], w_kbd=w_kbd[l], w_vbd=w_vbd[l], w_if=w_if[l], b_if=b_if[l],
             hn_g=hn_g[l], skip=skip[l], w_m_out=w_m_out[l], w_out=w_out[l], g_moe=g_moe[l],
             w_router=w_router[l], b_router=b_router[l], w_gu=w_gu[l], b_gu=b_gu[l],
             w_down=w_down[l], b_down=b_down[l], g_final=g_final.reshape(1, -1))
    return _layer(x, mod, p)
```

```python
import functools

import jax
import jax.numpy as jnp
from jax import lax
from jax.experimental import pallas as pl
from jax.experimental.pallas import tpu as pltpu

F32 = jnp.float32
BF16 = jnp.bfloat16
HIGHEST = lax.Precision.HIGHEST

D_MODEL = 1024
N_HEADS = 4
HEAD_DIM = 256
CHUNK = 128
CONV_WIDTH = 31
SC_WIDTH = 4
N_EXPERTS = 32
TOP_K = 4
SWIGLU_LIMIT = 7.0
SWIGLU_ALPHA = 1.702
EPS = 1e-6

LANES = 128
CONV_HALO = 32
SC_HALO = 16
VMEM_LIMIT = 56 * 1024 * 1024

TM_INPROJ = 512
TS_CONV = 512
TS_PRE = 512
TM_POST = 512
BM_EXPERT = 512
TR_ROWS = 256
TR_DISPATCH = 1024


def _cparams(sem, vmem=VMEM_LIMIT):
    return pltpu.CompilerParams(dimension_semantics=sem, vmem_limit_bytes=vmem)


def _sigmoid(v):
    return jax.nn.sigmoid(v)


def _ada_kernel(c_ref, w_ref, b_ref, o_ref):
    c = c_ref[...]
    s = c * _sigmoid(c)
    o_ref[...] = jnp.dot(s, w_ref[...], precision=HIGHEST, preferred_element_type=F32) + b_ref[...]


def _ada(c_pad, w_ada, b_ada):
    rows, d = c_pad.shape
    n = w_ada.shape[1]
    tn = 1024
    return pl.pallas_call(
        _ada_kernel,
        out_shape=jax.ShapeDtypeStruct((rows, n), F32),
        grid=(n // tn,),
        in_specs=[pl.BlockSpec((rows, d), lambda j: (0, 0)),
                  pl.BlockSpec((d, tn), lambda j: (0, j)),
                  pl.BlockSpec((1, tn), lambda j: (0, j))],
        out_specs=pl.BlockSpec((rows, tn), lambda j: (0, j)),
        compiler_params=_cparams(("arbitrary",)),
        name="ada",
    )(c_pad, w_ada, b_ada)


def _ada_norm(x, g, shift, scale):
    ms = jnp.mean(x * x, axis=-1, keepdims=True)
    y = x * lax.rsqrt(ms + EPS) * g
    return y * (1.0 + scale) + shift


def _inproj_kernel(x_ref, mod_ref, g_ref, w_ref, glu_ref, xm_ref, op_ref, za_ref, zb_ref):
    d = D_MODEL
    h = _ada_norm(x_ref[0], g_ref[...], mod_ref[0, 0:1, :], mod_ref[0, 1:2, :]).astype(BF16)

    def mm(k):
        return jnp.dot(h, w_ref[:, k * d:(k + 1) * d], preferred_element_type=F32)

    a = mm(0)
    glu_ref[0] = (a * _sigmoid(mm(1))).astype(BF16)
    xm_ref[0] = mm(2).astype(BF16)
    op_ref[0] = mm(3).astype(BF16)
    za_ref[0] = mm(4).astype(BF16)
    zb_ref[0] = mm(5).astype(BF16)


def _inproj(x, mod, g_mix, w_in_bf):
    b, s, d = x.shape
    tm = min(TM_INPROJ, s)
    tok = pl.BlockSpec((1, tm, d), lambda bi, i: (bi, i, 0))
    out = jax.ShapeDtypeStruct((b, s, d), BF16)
    return pl.pallas_call(
        _inproj_kernel,
        out_shape=[out] * 5,
        grid=(b, s // tm),
        in_specs=[tok,
                  pl.BlockSpec((1, 6, d), lambda bi, i: (bi, 0, 0)),
                  pl.BlockSpec((1, d), lambda bi, i: (0, 0)),
                  pl.BlockSpec(w_in_bf.shape, lambda bi, i: (0, 0))],
        out_specs=[tok] * 5,
        compiler_params=_cparams(("arbitrary", "arbitrary")),
        name="inproj",
    )(x, mod, g_mix, w_in_bf)


def _conv_kernel(u_ref, halo_ref, wdw_ref, bdw_ref, lng_ref, lnb_ref, o_ref, ubuf, ybuf):
    i = pl.program_id(1)
    ts = u_ref.shape[1]
    halo = halo_ref[0].astype(F32)
    ubuf[0:CONV_HALO, :] = jnp.where(i > 0, halo, 0.0)
    ubuf[CONV_HALO:CONV_HALO + ts, :] = u_ref[0].astype(F32)
    rb = 64
    first = CONV_HALO - (CONV_WIDTH - 1)

    def lane_tile(c, carry):
        cols = pl.ds(pl.multiple_of(c * LANES, LANES), LANES)
        bias = bdw_ref[:, cols]
        for r in range(ts // rb):
            acc = jnp.zeros((rb, LANES), F32) + bias
            for j in range(CONV_WIDTH):
                acc = acc + wdw_ref[j:j + 1, cols] * ubuf[r * rb + first + j:r * rb + first + j + rb, cols]
            ybuf[r * rb:(r + 1) * rb, cols] = acc
        return carry

    lax.fori_loop(0, D_MODEL // LANES, lane_tile, 0)
    y = ybuf[...]
    mu = jnp.mean(y, axis=-1, keepdims=True)
    yc = y - mu
    var = jnp.mean(yc * yc, axis=-1, keepdims=True)
    u = yc * lax.rsqrt(var + EPS) * lng_ref[...] + lnb_ref[...]
    o_ref[0] = (u * _sigmoid(u)).astype(BF16)


def _conv_branch(glu, w_dw, b_dw, ln_g, ln_b):
    b, s, d = glu.shape
    ts = min(TS_CONV, s)
    hb = ts // CONV_HALO
    tok = pl.BlockSpec((1, ts, d), lambda bi, i: (bi, i, 0))
    vec = pl.BlockSpec((1, d), lambda bi, i: (0, 0))
    return pl.pallas_call(
        _conv_kernel,
        out_shape=jax.ShapeDtypeStruct((b, s, d), BF16),
        grid=(b, s // ts),
        in_specs=[tok,
                  pl.BlockSpec((1, CONV_HALO, d), lambda bi, i: (bi, jnp.maximum(i * hb - 1, 0), 0)),
                  pl.BlockSpec((CONV_WIDTH, d), lambda bi, i: (0, 0)),
                  vec, vec, vec],
        out_specs=tok,
        scratch_shapes=[pltpu.VMEM((ts + CONV_HALO, d), F32), pltpu.VMEM((ts, d), F32)],
        compiler_params=_cparams(("arbitrary", "arbitrary")),
        name="conv",
    )(glu, glu, w_dw, b_dw, ln_g, ln_b)


def _mlstm_pre_kernel(xm_ref, halo_ref, wsc_ref, bsc_ref, bdq_ref, bdk_ref, bdkt_ref, bdv_ref,
                      wq_ref, wk_ref, wv_ref, bif_ref,
                      xc_ref, q_ref, kt_ref, v_ref, bc_ref, g_ref, xbuf):
    i = pl.program_id(1)
    ts = xm_ref.shape[1]
    xm_bf = xm_ref[0]
    xbuf[0:SC_HALO, :] = jnp.where(i > 0, halo_ref[0].astype(F32), 0.0)
    xbuf[SC_HALO:SC_HALO + ts, :] = xm_bf.astype(F32)
    first = SC_HALO - (SC_WIDTH - 1)
    y = jnp.zeros((ts, D_MODEL), F32) + bsc_ref[...]
    for j in range(SC_WIDTH):
        y = y + wsc_ref[j:j + 1, :] * xbuf[first + j:first + j + ts, :]
    xc = y * _sigmoid(y)
    xc_bf = xc.astype(BF16)
    xc_ref[0] = xc_bf

    gates = jnp.zeros((ts, 2 * LANES), F32) + bif_ref[...]
    for c in range(D_MODEL // LANES):
        cols = slice(c * LANES, (c + 1) * LANES)
        xcc = xc_bf[:, cols]
        qc = jnp.dot(xcc, bdq_ref[c], preferred_element_type=F32).astype(BF16)
        kc = jnp.dot(xcc, bdk_ref[c], preferred_element_type=F32).astype(BF16)
        vc = jnp.dot(xm_bf[:, cols], bdv_ref[c], preferred_element_type=F32).astype(BF16)
        ktc = lax.dot_general(bdkt_ref[c], xcc, (((1,), (1,)), ((), ())), preferred_element_type=F32)
        q_ref[0, :, cols] = qc
        v_ref[0, :, cols] = vc
        kt_ref[0, cols, :] = (ktc * (HEAD_DIM ** -0.5)).astype(BF16)
        gates = gates + jnp.dot(qc, wq_ref[cols, :], preferred_element_type=F32)
        gates = gates + jnp.dot(kc, wk_ref[cols, :], preferred_element_type=F32)
        gates = gates + jnp.dot(vc, wv_ref[cols, :], preferred_element_type=F32)

    ig = gates[:, :LANES]
    fpre = gates[:, LANES:]
    lane = lax.broadcasted_iota(jnp.int32, (ts, LANES), 1)
    logf = jnp.minimum(fpre, 0.0) - jnp.log(1.0 + jnp.exp(-jnp.abs(fpre)))
    logf = jnp.where(lane < N_HEADS, logf, 0.0)
    row = lax.broadcasted_iota(jnp.int32, (CHUNK, CHUNK), 0)
    col = lax.broadcasted_iota(jnp.int32, (CHUNK, CHUNK), 1)
    tri = (col <= row).astype(F32)
    for ch in range(ts // CHUNK):
        rows = slice(ch * CHUNK, (ch + 1) * CHUNK)
        bcum = jnp.dot(tri, logf[rows, :], precision=HIGHEST, preferred_element_type=F32)
        bc_ref[0, rows, :] = bcum
        gt = (ig[rows, :] - bcum).T
        g_ref[0, ch] = gt[0:8, :]


def _mlstm_pre(xm, w_sc, b_sc, bdq, bdk, bdkt, bdv, wq, wk, wv, bif):
    b, s, d = xm.shape
    ts = min(TS_PRE, s)
    hb = ts // SC_HALO
    nch = ts // CHUNK
    tok = pl.BlockSpec((1, ts, d), lambda bi, i: (bi, i, 0))
    vec = pl.BlockSpec((1, d), lambda bi, i: (0, 0))
    bd = pl.BlockSpec((d // LANES, LANES, LANES), lambda bi, i: (0, 0, 0))
    wg = pl.BlockSpec((d, 2 * LANES), lambda bi, i: (0, 0))
    tok_bf = jax.ShapeDtypeStruct((b, s, d), BF16)
    return pl.pallas_call(
        _mlstm_pre_kernel,
        out_shape=[tok_bf, tok_bf,
                   jax.ShapeDtypeStruct((b, d, s), BF16),
                   tok_bf,
                   jax.ShapeDtypeStruct((b, s, LANES), F32),
                   jax.ShapeDtypeStruct((b, s // CHUNK, 8, CHUNK), F32)],
        grid=(b, s // ts),
        in_specs=[tok,
                  pl.BlockSpec((1, SC_HALO, d), lambda bi, i: (bi, jnp.maximum(i * hb - 1, 0), 0)),
                  pl.BlockSpec((SC_WIDTH, d), lambda bi, i: (0, 0)),
                  vec, bd, bd, bd, bd, wg, wg, wg,
                  pl.BlockSpec((1, 2 * LANES), lambda bi, i: (0, 0))],
        out_specs=[tok, tok,
                   pl.BlockSpec((1, d, ts), lambda bi, i: (bi, 0, i)),
                   tok,
                   pl.BlockSpec((1, ts, LANES), lambda bi, i: (bi, i, 0)),
                   pl.BlockSpec((1, nch, 8, CHUNK), lambda bi, i: (bi, i, 0, 0))],
        scratch_shapes=[pltpu.VMEM((ts + SC_HALO, d), F32)],
        compiler_params=_cparams(("arbitrary", "arbitrary")),
        name="mlstm_pre",
    )(xm, xm, w_sc, b_sc, bdq, bdk, bdkt, bdv, wq, wk, wv, bif)


def _mlstm_kernel(q_ref, kt_ref, v_ref, bc_ref, g_ref, op_ref, xc_ref, hng_ref, skip_ref,
                  o_ref, c_scr, n_scr, m_scr):
    ci = pl.program_id(1)

    @pl.when(ci == 0)
    def _():
        c_scr[...] = jnp.zeros_like(c_scr)
        n_scr[...] = jnp.zeros_like(n_scr)
        m_scr[...] = jnp.zeros_like(m_scr)

    row = lax.broadcasted_iota(jnp.int32, (CHUNK, CHUNK), 0)
    col = lax.broadcasted_iota(jnp.int32, (CHUNK, CHUNK), 1)
    causal = col <= row
    for h in range(N_HEADS):
        hs = slice(h * HEAD_DIM, (h + 1) * HEAD_DIM)
        qh = q_ref[0, :, hs]
        kth = kt_ref[0, hs, :]
        vh = v_ref[0, :, hs]
        bcol = bc_ref[0, :, h:h + 1]
        grow = g_ref[0, 0, h:h + 1, :]
        m_prev = m_scr[h][:, 0:1]
        c_prev = c_scr[h]
        n_prev = n_scr[h]

        dlog = jnp.where(causal, bcol + grow, -jnp.inf)
        inter = bcol + m_prev
        m_t = jnp.maximum(inter, jnp.max(dlog, axis=-1, keepdims=True))
        w_intra = jnp.exp(dlog - m_t)
        w_inter = jnp.exp(inter - m_t)
        sc = jnp.dot(qh, kth, preferred_element_type=F32) * w_intra
        q_c = jnp.dot(qh, c_prev.astype(BF16), preferred_element_type=F32)
        q_n = jnp.dot(qh, n_prev.astype(BF16), preferred_element_type=F32)
        num = jnp.dot(sc.astype(BF16), vh, preferred_element_type=F32) + w_inter * q_c
        den = jnp.sum(sc, axis=-1, keepdims=True) + w_inter * q_n
        den = jnp.maximum(jnp.abs(den), jnp.exp(-m_t))
        hh = num / jnp.concatenate([den, den], axis=-1)

        hg = _sigmoid(op_ref[0, :, hs].astype(F32)) * hh
        mu = jnp.mean(hg, axis=-1, keepdims=True)
        hc = hg - mu
        var = jnp.mean(hc * hc, axis=-1, keepdims=True)
        hn = hc * lax.rsqrt(var + EPS) * hng_ref[:, hs]
        o_ref[0, :, hs] = (hn + skip_ref[:, hs] * xc_ref[0, :, hs].astype(F32)).astype(BF16)

        b_last = bcol[CHUNK - 1:CHUNK, :]
        m_new = jnp.maximum(b_last + m_prev, b_last + jnp.max(grow, axis=-1, keepdims=True))
        decay = jnp.exp(b_last + m_prev - m_new)
        wrow = jnp.exp(b_last + grow - m_new)
        wkt = kth.astype(F32) * wrow
        c_scr[h] = decay * c_prev + jnp.dot(wkt.astype(BF16), vh, preferred_element_type=F32)
        n_scr[h] = decay * n_prev + jnp.sum(wkt, axis=-1, keepdims=True)
        m_scr[h] = jnp.broadcast_to(m_new, (1, LANES))


def _mlstm(q, kt, v, bc, g, o_pre, xc, hn_g, skip):
    b, s, d = q.shape
    nc = s // CHUNK
    tok = pl.BlockSpec((1, CHUNK, d), lambda bi, ci: (bi, ci, 0))
    vec = pl.BlockSpec((1, d), lambda bi, ci: (0, 0))
    return pl.pallas_call(
        _mlstm_kernel,
        out_shape=jax.ShapeDtypeStruct((b, s, d), BF16),
        grid=(b, nc),
        in_specs=[tok,
                  pl.BlockSpec((1, d, CHUNK), lambda bi, ci: (bi, 0, ci)),
                  tok,
                  pl.BlockSpec((1, CHUNK, LANES), lambda bi, ci: (bi, ci, 0)),
                  pl.BlockSpec((1, 1, 8, CHUNK), lambda bi, ci: (bi, ci, 0, 0)),
                  tok, tok, vec, vec],
        out_specs=tok,
        scratch_shapes=[pltpu.VMEM((N_HEADS, HEAD_DIM, HEAD_DIM), F32),
                        pltpu.VMEM((N_HEADS, HEAD_DIM, LANES), F32),
                        pltpu.VMEM((N_HEADS, 1, LANES), F32)],
        compiler_params=_cparams(("arbitrary", "arbitrary")),
        name="mlstm",
    )(q, kt, v, bc, g, o_pre, xc, hn_g, skip)


def _post_kernel(x_ref, aa_ref, mo_ref, za_ref, zb_ref, mod_ref, wpw_ref, bpw_ref, wmo_ref, wout_ref,
                 gmoe_ref, wr_ref, br_ref,
                 x1_ref, hm_ref, mi_ref, mw_ref, cnt_ref, carry):
    first = (pl.program_id(0) == 0) & (pl.program_id(1) == 0)

    @pl.when(first)
    def _():
        carry[...] = jnp.zeros_like(carry)

    tm = x_ref.shape[1]
    y_a = jnp.dot(aa_ref[0], wpw_ref[...], preferred_element_type=F32) + bpw_ref[...]
    y_b = jnp.dot(mo_ref[0], wmo_ref[...], preferred_element_type=F32)
    merged = _sigmoid(za_ref[0].astype(F32)) * y_a + _sigmoid(zb_ref[0].astype(F32)) * y_b
    mixed = jnp.dot(merged.astype(BF16), wout_ref[...], preferred_element_type=F32)
    x1 = x_ref[0] + mod_ref[0, 2:3, :] * mixed
    x1_ref[0] = x1
    hm = _ada_norm(x1, gmoe_ref[...], mod_ref[0, 3:4, :], mod_ref[0, 4:5, :])
    hm_ref[0] = hm

    logits = jnp.dot(hm, wr_ref[...], precision=HIGHEST, preferred_element_type=F32) + br_ref[...]
    lane = lax.broadcasted_iota(jnp.int32, (tm, LANES), 1)
    work = jnp.where(lane < N_EXPERTS, logits, -jnp.inf)
    vals, idxs = [], []
    for _ in range(TOP_K):
        mx = jnp.max(work, axis=-1, keepdims=True)
        ix = jnp.min(jnp.where(work == mx, lane, LANES), axis=-1, keepdims=True)
        vals.append(mx)
        idxs.append(ix)
        work = jnp.where(lane == ix, -jnp.inf, work)
    exps = [jnp.exp(v - vals[0]) for v in vals]
    tot = exps[0] + exps[1] + exps[2] + exps[3]

    onehots = [lane == ix for ix in idxs]
    occ = jnp.zeros((tm, LANES), F32)
    for oh in onehots:
        occ = occ + oh.astype(F32)
    r_i = lax.broadcasted_iota(jnp.int32, (tm, tm), 0)
    c_i = lax.broadcasted_iota(jnp.int32, (tm, tm), 1)
    strict = (c_i < r_i).astype(BF16)
    before = jnp.dot(strict, occ.astype(BF16), preferred_element_type=F32) + carry[...]
    mi = jnp.zeros((tm, LANES), jnp.int32)
    mw = jnp.zeros((tm, LANES), F32)
    for j in range(TOP_K):
        rank = jnp.sum(jnp.where(onehots[j], before, 0.0), axis=-1, keepdims=True).astype(jnp.int32)
        mi = jnp.where(lane == j, idxs[j], mi)
        mi = jnp.where(lane == TOP_K + j, rank, mi)
        mw = jnp.where(lane == j, exps[j] / tot, mw)
    mi_ref[0] = mi
    mw_ref[0] = mw
    total = carry[...] + jnp.sum(occ, axis=0, keepdims=True)
    carry[...] = total
    cnt_ref[...] = jnp.broadcast_to(total, cnt_ref.shape)


def _post(x, act_a, mo_in, z_a, z_b, mod, w_pw, b_pw, w_mo, w_out, g_moe, w_r, b_r):
    b, s, d = x.shape
    tm = min(TM_POST, s)
    tok = pl.BlockSpec((1, tm, d), lambda bi, i: (bi, i, 0))
    meta = pl.BlockSpec((1, tm, LANES), lambda bi, i: (bi, i, 0))
    vec = pl.BlockSpec((1, d), lambda bi, i: (0, 0))
    mat = pl.BlockSpec((d, d), lambda bi, i: (0, 0))
    return pl.pallas_call(
        _post_kernel,
        out_shape=[jax.ShapeDtypeStruct((b, s, d), F32),
                   jax.ShapeDtypeStruct((b, s, d), F32),
                   jax.ShapeDtypeStruct((b, s, LANES), jnp.int32),
                   jax.ShapeDtypeStruct((b, s, LANES), F32),
                   jax.ShapeDtypeStruct((8, LANES), F32)],
        grid=(b, s // tm),
        in_specs=[tok, tok, tok, tok, tok,
                  pl.BlockSpec((1, 6, d), lambda bi, i: (bi, 0, 0)),
                  mat, vec, mat, mat, vec,
                  pl.BlockSpec((d, LANES), lambda bi, i: (0, 0)),
                  pl.BlockSpec((1, LANES), lambda bi, i: (0, 0))],
        out_specs=[tok, tok, meta, meta, pl.BlockSpec((8, LANES), lambda bi, i: (0, 0))],
        scratch_shapes=[pltpu.VMEM((1, LANES), F32)],
        compiler_params=_cparams(("arbitrary", "arbitrary")),
        name="post",
    )(x, act_a, mo_in, z_a, z_b, mod, w_pw, b_pw, w_mo, w_out, g_moe, w_r, b_r)


def _dispatch_kernel(zlist_ref, dest_ref, hm_ref, xs_hbm, zbuf, sem, zsem):
    i = pl.program_id(0)

    @pl.when(i == 0)
    def _():
        zbuf[...] = jnp.zeros_like(zbuf)

        def zero_copy(blk):
            return pltpu.make_async_copy(zbuf, xs_hbm.at[pl.ds(blk * BM_EXPERT, BM_EXPERT)], zsem)

        def start(k, carry):
            @pl.when(zlist_ref[k] >= 0)
            def _():
                zero_copy(zlist_ref[k]).start()
            return carry

        def wait(k, carry):
            @pl.when(zlist_ref[k] >= 0)
            def _():
                zero_copy(zlist_ref[k]).wait()
            return carry

        lax.fori_loop(0, zlist_ref.shape[0], start, 0)
        lax.fori_loop(0, zlist_ref.shape[0], wait, 0)

    rows = hm_ref.shape[0]

    def body(r, carry):
        for j in range(TOP_K):
            dst = dest_ref[TOP_K * r + j]
            pltpu.make_async_copy(hm_ref.at[pl.ds(r, 1)], xs_hbm.at[pl.ds(dst, 1)], sem).start()
        return carry

    lax.fori_loop(0, rows, body, 0)
    for j in range(TOP_K):
        pltpu.make_async_copy(hm_ref, xs_hbm.at[pl.ds(0, rows)], sem).wait()


def _dispatch(zlist, dest_flat, hm_flat, n_rows):
    t, d = hm_flat.shape
    tr = min(TR_DISPATCH, t)
    blk = TOP_K * tr
    grid_spec = pltpu.PrefetchScalarGridSpec(
        num_scalar_prefetch=1,
        grid=(t // tr,),
        in_specs=[pl.BlockSpec((blk,), lambda i, zl: (i,), memory_space=pltpu.SMEM),
                  pl.BlockSpec((tr, d), lambda i, zl: (i, 0))],
        out_specs=pl.BlockSpec(memory_space=pl.ANY),
        scratch_shapes=[pltpu.VMEM((BM_EXPERT, d), F32),
                        pltpu.SemaphoreType.DMA(()), pltpu.SemaphoreType.DMA(())],
    )
    return pl.pallas_call(
        _dispatch_kernel,
        out_shape=jax.ShapeDtypeStruct((n_rows, d), F32),
        grid_spec=grid_spec,
        compiler_params=_cparams(("arbitrary",)),
        name="dispatch",
    )(zlist, dest_flat, hm_flat)


def _expert_kernel(be_ref, nu_ref, xs_ref, wgu_ref, bgu_ref, wdn_ref, bdn_ref, ys_ref, wgu_bf, wdn_bf):
    i = pl.program_id(0)
    d = D_MODEL
    prev = be_ref[jnp.maximum(i - 1, 0)]
    changed = (i == 0) | (be_ref[i] != prev)

    @pl.when(changed)
    def _():
        wgu_bf[...] = wgu_ref[0].astype(BF16)
        wdn_bf[...] = wdn_ref[0].astype(BF16)

    @pl.when(i < nu_ref[0])
    def _():
        x = xs_ref[...].astype(BF16)
        acc = jnp.zeros(ys_ref.shape, F32) + bdn_ref[0]
        ck = 512
        for n in range(d // ck):
            cg = slice(n * ck, (n + 1) * ck)
            cu = slice(d + n * ck, d + (n + 1) * ck)
            gate = jnp.dot(x, wgu_bf[:, cg], preferred_element_type=F32) + bgu_ref[0, :, cg]
            up = jnp.dot(x, wgu_bf[:, cu], preferred_element_type=F32) + bgu_ref[0, :, cu]
            gate = jnp.minimum(gate, SWIGLU_LIMIT)
            up = jnp.clip(up, -SWIGLU_LIMIT, SWIGLU_LIMIT)
            glu = gate * _sigmoid(gate * SWIGLU_ALPHA)
            act = ((up + 1.0) * glu).astype(BF16)
            acc = acc + jnp.dot(act, wdn_bf[cg, :], preferred_element_type=F32)
        ys_ref[...] = acc

    @pl.when(i >= nu_ref[0])
    def _():
        ys_ref[...] = jnp.zeros_like(ys_ref)


def _experts(blk_e, n_used, xs, w_gu, b_gu, w_dn, b_dn):
    p, d = xs.shape
    bm = BM_EXPERT
    e = w_gu.shape[0]
    grid_spec = pltpu.PrefetchScalarGridSpec(
        num_scalar_prefetch=2,
        grid=(p // bm,),
        in_specs=[pl.BlockSpec((bm, d), lambda i, be, nu: (i, 0)),
                  pl.BlockSpec((1, d, 2 * d), lambda i, be, nu: (be[i], 0, 0)),
                  pl.BlockSpec((1, 1, 2 * d), lambda i, be, nu: (be[i], 0, 0)),
                  pl.BlockSpec((1, d, d), lambda i, be, nu: (be[i], 0, 0)),
                  pl.BlockSpec((1, 1, d), lambda i, be, nu: (be[i], 0, 0))],
        out_specs=pl.BlockSpec((bm, d), lambda i, be, nu: (i, 0)),
        scratch_shapes=[pltpu.VMEM((d, 2 * d), BF16), pltpu.VMEM((d, d), BF16)],
    )
    return pl.pallas_call(
        _expert_kernel,
        out_shape=jax.ShapeDtypeStruct((p, d), F32),
        grid_spec=grid_spec,
        compiler_params=_cparams(("arbitrary",)),
        name="experts",
    )(blk_e, n_used, xs, w_gu, b_gu.reshape(e, 1, 2 * d), w_dn, b_dn.reshape(e, 1, d))


def _combine_kernel(dcur_ref, dnext_ref, ys_hbm, x1_ref, mw_ref, mod_ref, gf_ref, o_ref, gbuf, sem):
    i = pl.program_id(0)
    n = pl.num_programs(0)
    slot = i % 2

    def issue(dref, sl):
        def body(r, carry):
            for j in range(TOP_K):
                src = dref[TOP_K * r + j]
                pltpu.make_async_copy(ys_hbm.at[pl.ds(src, 1)], gbuf.at[sl, j, pl.ds(r, 1)],
                                      sem.at[sl]).start()
            return carry
        lax.fori_loop(0, TR_ROWS, body, 0)

    @pl.when(i == 0)
    def _():
        issue(dcur_ref, 0)

    @pl.when(i + 1 < n)
    def _():
        issue(dnext_ref, 1 - slot)

    for j in range(TOP_K):
        pltpu.make_async_copy(ys_hbm.at[pl.ds(0, TR_ROWS)], gbuf.at[slot, j], sem.at[slot]).wait()

    moe = jnp.zeros((TR_ROWS, D_MODEL), F32)
    for j in range(TOP_K):
        moe = moe + mw_ref[:, j:j + 1] * gbuf[slot, j]
    xo = x1_ref[...] + mod_ref[0, 5:6, :] * moe
    ms = jnp.mean(xo * xo, axis=-1, keepdims=True)
    o_ref[...] = xo * lax.rsqrt(ms + EPS) * gf_ref[...]


def _combine(dest_flat, ys, x1_flat, mw_flat, mod, g_final, seq):
    t, d = x1_flat.shape
    blk = TOP_K * TR_ROWS
    nsteps = t // TR_ROWS
    per_seq = seq // TR_ROWS
    return pl.pallas_call(
        _combine_kernel,
        out_shape=jax.ShapeDtypeStruct((t, d), F32),
        grid=(nsteps,),
        in_specs=[pl.BlockSpec((blk,), lambda i: (i,), memory_space=pltpu.SMEM),
                  pl.BlockSpec((blk,), lambda i: (jnp.minimum(i + 1, nsteps - 1),), memory_space=pltpu.SMEM),
                  pl.BlockSpec(memory_space=pl.ANY),
                  pl.BlockSpec((TR_ROWS, d), lambda i: (i, 0)),
                  pl.BlockSpec((TR_ROWS, LANES), lambda i: (i, 0)),
                  pl.BlockSpec((1, 6, d), lambda i: (i // per_seq, 0, 0)),
                  pl.BlockSpec((1, d), lambda i: (0, 0))],
        out_specs=pl.BlockSpec((TR_ROWS, d), lambda i: (i, 0)),
        scratch_shapes=[pltpu.VMEM((2, TOP_K, TR_ROWS, d), F32), pltpu.SemaphoreType.DMA((2,))],
        compiler_params=_cparams(("arbitrary",)),
        name="combine",
    )(dest_flat, dest_flat, ys, x1_flat, mw_flat, mod, g_final)


def _blockdiag_tiles(w):
    nb = w.shape[0]
    per = LANES // 4
    wt = w.reshape(nb // per, per, 4, 4)
    eye = jnp.eye(per, dtype=w.dtype)
    return jnp.einsum('cnio,nm->cnimo', wt, eye).reshape(nb // per, LANES, LANES)


def _gate_weights(w_if_part):
    out = jnp.zeros((w_if_part.shape[0], 2 * LANES), F32)
    out = out.at[:, :N_HEADS].set(w_if_part[:, :N_HEADS])
    out = out.at[:, LANES:LANES + N_HEADS].set(w_if_part[:, N_HEADS:])
    return out.astype(BF16)


def _layer(x, mod, p):
    b, s, d = x.shape
    t = b * s
    row = lambda v: v.reshape(1, -1)

    glu, xm, o_pre, z_a, z_b = _inproj(x, mod, row(p['g_mix']), p['w_in'].astype(BF16))
    act_a = _conv_branch(glu, p['w_dw'], row(p['b_dw']), row(p['ln_g']), row(p['ln_b']))

    bdq = _blockdiag_tiles(p['w_qbd']).astype(BF16)
    bdk = _blockdiag_tiles(p['w_kbd']).astype(BF16)
    bdv = _blockdiag_tiles(p['w_vbd']).astype(BF16)
    bdkt = jnp.swapaxes(bdk, 1, 2)
    w_if = p['w_if']
    bif = jnp.zeros((1, 2 * LANES), F32)
    bif = bif.at[0, :N_HEADS].set(p['b_if'][:N_HEADS]).at[0, LANES:LANES + N_HEADS].set(p['b_if'][N_HEADS:])
    xc, q, kt, v, bc, g = _mlstm_pre(xm, p['w_sc'], row(p['b_sc']), bdq, bdk, bdkt, bdv,
                                     _gate_weights(w_if[:d]), _gate_weights(w_if[d:2 * d]),
                                     _gate_weights(w_if[2 * d:]), bif)
    mo_in = _mlstm(q, kt, v, bc, g, o_pre, xc, row(p['hn_g']), row(p['skip']))

    w_r = jnp.zeros((d, LANES), F32).at[:, :N_EXPERTS].set(p['w_router'])
    b_r = jnp.zeros((1, LANES), F32).at[0, :N_EXPERTS].set(p['b_router'])
    x1, hm, mi, mw, cnt = _post(x, act_a, mo_in, z_a, z_b, mod,
                                p['w_conv_out'].astype(BF16), row(p['b_conv_out']),
                                p['w_m_out'].astype(BF16), p['w_out'].astype(BF16),
                                row(p['g_moe']), w_r, b_r)

    bm = BM_EXPERT
    counts = cnt[0, :N_EXPERTS].astype(jnp.int32)
    padded = (counts + bm - 1) // bm * bm
    pad_end = jnp.cumsum(padded)
    start_p = pad_end - padded
    n_rows = t * TOP_K + N_EXPERTS * bm
    nblk = n_rows // bm
    blk_start = jnp.arange(nblk, dtype=jnp.int32) * bm
    blk_e = jnp.minimum(jnp.sum((pad_end[None, :] <= blk_start[:, None]).astype(jnp.int32), axis=1),
                        N_EXPERTS - 1)
    n_used = (pad_end[-1:] // bm).astype(jnp.int32)
    mi = mi.reshape(t, LANES)
    dest = (start_p[mi[:, :TOP_K]] + mi[:, TOP_K:2 * TOP_K]).reshape(t * TOP_K)

    last_blk = jnp.where(padded > 0, pad_end // bm - 1, -1)
    tail_blk = n_used[0] + jnp.arange(N_EXPERTS, dtype=jnp.int32)
    zlist = jnp.concatenate([last_blk, jnp.where(tail_blk < nblk, tail_blk, -1)]).astype(jnp.int32)

    xs = _dispatch(zlist, dest, hm.reshape(t, d), n_rows)
    ys = _experts(blk_e, n_used, xs, p['w_gu'], p['b_gu'], p['w_down'], p['b_down'])
    out = _combine(dest, ys, x1.reshape(t, d), mw.reshape(t, LANES), mod, p['g_final'], s)
    return out.reshape(b, s, d)


def kernel(x, c, w_ada, b_ada, g_mix, w_in, w_dw, b_dw, ln_g, ln_b, w_conv_out, b_conv_out, w_sc, b_sc,
           w_qbd, w_kbd, w_vbd, w_if, b_if, hn_g, skip, w_m_out, w_out, g_moe, w_router, b_router,
           w_gu, b_gu, w_down, b_down, g_final):
    b, s, d = x.shape
    depth = w_ada.shape[0]
    assert depth == 1 and d == D_MODEL and s % 512 == 0
    rows = -(-b // 8) * 8
    c_pad = jnp.zeros((rows, d), F32).at[:b].set(c)
    l = 0
    mod = _ada(c_pad, w_ada[l], b_ada[l].reshape(1, -1))[:b].reshape(b, 6, d)
    p = dict(g_mix=g_mix[l], w_in=w_in[l], w_dw=w_dw[l], b_dw=b_dw[l], ln_g=ln_g[l], ln_b=ln_b[l],
             w_conv_out=w_conv_out[l], b_conv_out=b_conv_out[l], w_sc=w_sc[l], b_sc=b_sc[l],
             w_qbd=w_qbd[l], w_kbd=w_kbd[l], w_vbd=w_vbd[l], w_if=w_if[l], b_if=b_if[l],
             hn_g=hn_g[l], skip=skip[l], w_m_out=w_m_out[l], w_out=w_out[l], g_moe=g_moe[l],
             w_router=w_router[l], b_router=b_router[l], w_gu=w_gu[l], b_gu=b_gu[l],
             w_down=w_down[l], b_down=b_down[l], g_final=g_final.reshape(1, -1))
    return _layer(x, mod, p)
```

```python
import functools

import jax
import jax.numpy as jnp
from jax import lax
from jax.experimental import pallas as pl
from jax.experimental.pallas import tpu as pltpu

F32 = jnp.float32
BF16 = jnp.bfloat16
HIGHEST = lax.Precision.HIGHEST

D_MODEL = 1024
N_HEADS = 4
HEAD_DIM = 256
CHUNK = 128
CONV_WIDTH = 31
SC_WIDTH = 4
N_EXPERTS = 32
TOP_K = 4
SWIGLU_LIMIT = 7.0
SWIGLU_ALPHA = 1.702
EPS = 1e-6

LANES = 128
CONV_HALO = 32
SC_HALO = 16
VMEM_LIMIT = 56 * 1024 * 1024

TM_INPROJ = 512
TS_CONV = 512
TS_PRE = 512
TM_POST = 512
BM_EXPERT = 512
TR_ROWS = 256
TR_DISPATCH = 1024
NB_MLSTM = 4


def _cparams(sem, vmem=VMEM_LIMIT):
    return pltpu.CompilerParams(dimension_semantics=sem, vmem_limit_bytes=vmem)


def _sigmoid(v):
    return jax.nn.sigmoid(v)


def _ada_kernel(c_ref, w_ref, b_ref, o_ref):
    c = c_ref[...]
    s = c * _sigmoid(c)
    o_ref[...] = jnp.dot(s, w_ref[...], precision=HIGHEST, preferred_element_type=F32) + b_ref[...]


def _ada(c_pad, w_ada, b_ada):
    rows, d = c_pad.shape
    n = w_ada.shape[1]
    tn = 1024
    return pl.pallas_call(
        _ada_kernel,
        out_shape=jax.ShapeDtypeStruct((rows, n), F32),
        grid=(n // tn,),
        in_specs=[pl.BlockSpec((rows, d), lambda j: (0, 0)),
                  pl.BlockSpec((d, tn), lambda j: (0, j)),
                  pl.BlockSpec((1, tn), lambda j: (0, j))],
        out_specs=pl.BlockSpec((rows, tn), lambda j: (0, j)),
        compiler_params=_cparams(("arbitrary",)),
        name="ada",
    )(c_pad, w_ada, b_ada)


def _ada_norm(x, g, shift, scale):
    ms = jnp.mean(x * x, axis=-1, keepdims=True)
    y = x * lax.rsqrt(ms + EPS) * g
    return y * (1.0 + scale) + shift


def _inproj_kernel(x_ref, mod_ref, g_ref, w_ref, glu_ref, xm_ref, op_ref, za_ref, zb_ref):
    d = D_MODEL
    h = _ada_norm(x_ref[0], g_ref[...], mod_ref[0, 0:1, :], mod_ref[0, 1:2, :]).astype(BF16)

    def mm(k):
        return jnp.dot(h, w_ref[:, k * d:(k + 1) * d], preferred_element_type=F32)

    a = mm(0)
    glu_ref[0] = (a * _sigmoid(mm(1))).astype(BF16)
    xm_ref[0] = mm(2).astype(BF16)
    op_ref[0] = mm(3).astype(BF16)
    za_ref[0] = mm(4).astype(BF16)
    zb_ref[0] = mm(5).astype(BF16)


def _inproj(x, mod, g_mix, w_in_bf):
    b, s, d = x.shape
    tm = min(TM_INPROJ, s)
    tok = pl.BlockSpec((1, tm, d), lambda bi, i: (bi, i, 0))
    out = jax.ShapeDtypeStruct((b, s, d), BF16)
    return pl.pallas_call(
        _inproj_kernel,
        out_shape=[out] * 5,
        grid=(b, s // tm),
        in_specs=[tok,
                  pl.BlockSpec((1, 6, d), lambda bi, i: (bi, 0, 0)),
                  pl.BlockSpec((1, d), lambda bi, i: (0, 0)),
                  pl.BlockSpec(w_in_bf.shape, lambda bi, i: (0, 0))],
        out_specs=[tok] * 5,
        compiler_params=_cparams(("arbitrary", "arbitrary")),
        name="inproj",
    )(x, mod, g_mix, w_in_bf)


def _conv_kernel(u_ref, halo_ref, wdw_ref, bdw_ref, lng_ref, lnb_ref, o_ref, ubuf, ybuf):
    i = pl.program_id(1)
    ts = u_ref.shape[1]
    halo = halo_ref[0].astype(F32)
    ubuf[0:CONV_HALO, :] = jnp.where(i > 0, halo, 0.0)
    ubuf[CONV_HALO:CONV_HALO + ts, :] = u_ref[0].astype(F32)
    rb = 64
    first = CONV_HALO - (CONV_WIDTH - 1)

    groups = [[j for j in range(CONV_WIDTH) if (first + j) % 8 == b] for b in range(8)]

    def lane_tile(c, carry):
        cols = pl.ds(pl.multiple_of(c * LANES, LANES), LANES)
        bias = bdw_ref[:, cols]
        for r in range(ts // rb):
            acc = jnp.zeros((rb, LANES), F32) + bias
            for b, taps in enumerate(groups):
                ext = rb + (8 if b else 0)
                part = None
                for j in taps:
                    row0 = r * rb + (first + j) // 8 * 8
                    term = wdw_ref[j:j + 1, cols] * ubuf[row0:row0 + ext, cols]
                    part = term if part is None else part + term
                if part is not None:
                    acc = acc + part[b:b + rb, :]
            ybuf[r * rb:(r + 1) * rb, cols] = acc
        return carry

    lax.fori_loop(0, D_MODEL // LANES, lane_tile, 0)
    y = ybuf[...]
    mu = jnp.mean(y, axis=-1, keepdims=True)
    yc = y - mu
    var = jnp.mean(yc * yc, axis=-1, keepdims=True)
    u = yc * lax.rsqrt(var + EPS) * lng_ref[...] + lnb_ref[...]
    o_ref[0] = (u * _sigmoid(u)).astype(BF16)


def _conv_branch(glu, w_dw, b_dw, ln_g, ln_b):
    b, s, d = glu.shape
    ts = min(TS_CONV, s)
    hb = ts // CONV_HALO
    tok = pl.BlockSpec((1, ts, d), lambda bi, i: (bi, i, 0))
    vec = pl.BlockSpec((1, d), lambda bi, i: (0, 0))
    return pl.pallas_call(
        _conv_kernel,
        out_shape=jax.ShapeDtypeStruct((b, s, d), BF16),
        grid=(b, s // ts),
        in_specs=[tok,
                  pl.BlockSpec((1, CONV_HALO, d), lambda bi, i: (bi, jnp.maximum(i * hb - 1, 0), 0)),
                  pl.BlockSpec((CONV_WIDTH, d), lambda bi, i: (0, 0)),
                  vec, vec, vec],
        out_specs=tok,
        scratch_shapes=[pltpu.VMEM((ts + CONV_HALO, d), F32), pltpu.VMEM((ts, d), F32)],
        compiler_params=_cparams(("arbitrary", "arbitrary")),
        name="conv",
    )(glu, glu, w_dw, b_dw, ln_g, ln_b)


def _mlstm_pre_kernel(xm_ref, halo_ref, wsc_ref, bsc_ref, bdq_ref, bdk_ref, bdkt_ref, bdv_ref,
                      wq_ref, wk_ref, wv_ref, bif_ref,
                      xc_ref, q_ref, kt_ref, v_ref, bc_ref, g_ref, xbuf):
    i = pl.program_id(1)
    ts = xm_ref.shape[1]
    xm_bf = xm_ref[0]
    xbuf[0:SC_HALO, :] = jnp.where(i > 0, halo_ref[0].astype(F32), 0.0)
    xbuf[SC_HALO:SC_HALO + ts, :] = xm_bf.astype(F32)
    first = SC_HALO - (SC_WIDTH - 1)
    y = jnp.zeros((ts, D_MODEL), F32) + bsc_ref[...]
    for j in range(SC_WIDTH):
        y = y + wsc_ref[j:j + 1, :] * xbuf[first + j:first + j + ts, :]
    xc = y * _sigmoid(y)
    xc_bf = xc.astype(BF16)
    xc_ref[0] = xc_bf

    gates = jnp.zeros((ts, 2 * LANES), F32) + bif_ref[...]
    for c in range(D_MODEL // LANES):
        cols = slice(c * LANES, (c + 1) * LANES)
        xcc = xc_bf[:, cols]
        qc = jnp.dot(xcc, bdq_ref[c], preferred_element_type=F32).astype(BF16)
        kc = jnp.dot(xcc, bdk_ref[c], preferred_element_type=F32).astype(BF16)
        vc = jnp.dot(xm_bf[:, cols], bdv_ref[c], preferred_element_type=F32).astype(BF16)
        ktc = lax.dot_general(bdkt_ref[c], xcc, (((1,), (1,)), ((), ())), preferred_element_type=F32)
        q_ref[0, :, cols] = qc
        v_ref[0, :, cols] = vc
        kt_ref[0, cols, :] = (ktc * (HEAD_DIM ** -0.5)).astype(BF16)
        gates = gates + jnp.dot(qc, wq_ref[cols, :], preferred_element_type=F32)
        gates = gates + jnp.dot(kc, wk_ref[cols, :], preferred_element_type=F32)
        gates = gates + jnp.dot(vc, wv_ref[cols, :], preferred_element_type=F32)

    ig = gates[:, :LANES]
    fpre = gates[:, LANES:]
    lane = lax.broadcasted_iota(jnp.int32, (ts, LANES), 1)
    logf = jnp.minimum(fpre, 0.0) - jnp.log(1.0 + jnp.exp(-jnp.abs(fpre)))
    logf = jnp.where(lane < N_HEADS, logf, 0.0)
    row = lax.broadcasted_iota(jnp.int32, (CHUNK, CHUNK), 0)
    col = lax.broadcasted_iota(jnp.int32, (CHUNK, CHUNK), 1)
    tri = (col <= row).astype(F32)
    for ch in range(ts // CHUNK):
        rows = slice(ch * CHUNK, (ch + 1) * CHUNK)
        bcum = jnp.dot(tri, logf[rows, :], precision=HIGHEST, preferred_element_type=F32)
        bc_ref[0, rows, :] = bcum
        gt = (ig[rows, :] - bcum).T
        g_ref[0, ch] = gt[0:8, :]


def _mlstm_pre(xm, w_sc, b_sc, bdq, bdk, bdkt, bdv, wq, wk, wv, bif):
    b, s, d = xm.shape
    ts = min(TS_PRE, s)
    hb = ts // SC_HALO
    nch = ts // CHUNK
    tok = pl.BlockSpec((1, ts, d), lambda bi, i: (bi, i, 0))
    vec = pl.BlockSpec((1, d), lambda bi, i: (0, 0))
    bd = pl.BlockSpec((d // LANES, LANES, LANES), lambda bi, i: (0, 0, 0))
    wg = pl.BlockSpec((d, 2 * LANES), lambda bi, i: (0, 0))
    tok_bf = jax.ShapeDtypeStruct((b, s, d), BF16)
    return pl.pallas_call(
        _mlstm_pre_kernel,
        out_shape=[tok_bf, tok_bf,
                   jax.ShapeDtypeStruct((b, d, s), BF16),
                   tok_bf,
                   jax.ShapeDtypeStruct((b, s, LANES), F32),
                   jax.ShapeDtypeStruct((b, s // CHUNK, 8, CHUNK), F32)],
        grid=(b, s // ts),
        in_specs=[tok,
                  pl.BlockSpec((1, SC_HALO, d), lambda bi, i: (bi, jnp.maximum(i * hb - 1, 0), 0)),
                  pl.BlockSpec((SC_WIDTH, d), lambda bi, i: (0, 0)),
                  vec, bd, bd, bd, bd, wg, wg, wg,
                  pl.BlockSpec((1, 2 * LANES), lambda bi, i: (0, 0))],
        out_specs=[tok, tok,
                   pl.BlockSpec((1, d, ts), lambda bi, i: (bi, 0, i)),
                   tok,
                   pl.BlockSpec((1, ts, LANES), lambda bi, i: (bi, i, 0)),
                   pl.BlockSpec((1, nch, 8, CHUNK), lambda bi, i: (bi, i, 0, 0))],
        scratch_shapes=[pltpu.VMEM((ts + SC_HALO, d), F32)],
        compiler_params=_cparams(("arbitrary", "arbitrary")),
        name="mlstm_pre",
    )(xm, xm, w_sc, b_sc, bdq, bdk, bdkt, bdv, wq, wk, wv, bif)


def _mlstm_kernel(q_ref, kt_ref, v_ref, bc_ref, g_ref, op_ref, xc_ref, hng_ref, skip_ref,
                  o_ref, c_scr, n_scr, m_scr):
    ci = pl.program_id(1)

    @pl.when(ci == 0)
    def _():
        c_scr[...] = jnp.zeros_like(c_scr)
        n_scr[...] = jnp.zeros_like(n_scr)
        m_scr[...] = jnp.zeros_like(m_scr)

    row = lax.broadcasted_iota(jnp.int32, (CHUNK, CHUNK), 0)
    col = lax.broadcasted_iota(jnp.int32, (CHUNK, CHUNK), 1)
    causal = col <= row
    chains = [divmod(sh, N_HEADS) for sh in range(q_ref.shape[0] * N_HEADS)]
    hsl = lambda h: slice(h * HEAD_DIM, (h + 1) * HEAD_DIM)

    gate = []
    for sh, (bi, h) in enumerate(chains):
        bcol = bc_ref[bi, :, h:h + 1]
        grow = g_ref[bi, 0, h:h + 1, :]
        m_prev = m_scr[sh][:, 0:1]
        dlog = jnp.where(causal, bcol + grow, -jnp.inf)
        inter = bcol + m_prev
        m_t = jnp.maximum(inter, jnp.max(dlog, axis=-1, keepdims=True))
        b_last = bcol[CHUNK - 1:CHUNK, :]
        m_new = jnp.maximum(b_last + m_prev, b_last + jnp.max(grow, axis=-1, keepdims=True))
        gate.append(dict(w_intra=jnp.exp(dlog - m_t), w_inter=jnp.exp(inter - m_t), emt=jnp.exp(-m_t),
                         decay=jnp.exp(b_last + m_prev - m_new), wrow=jnp.exp(b_last + grow - m_new),
                         m_new=m_new))

    prods = []
    for sh, (bi, h) in enumerate(chains):
        qh = q_ref[bi, :, hsl(h)]
        kth = kt_ref[bi, hsl(h), :]
        sc = jnp.dot(qh, kth, preferred_element_type=F32) * gate[sh]['w_intra']
        q_c = jnp.dot(qh, c_scr[sh].astype(BF16), preferred_element_type=F32)
        q_n = jnp.dot(qh, n_scr[sh].astype(BF16), preferred_element_type=F32)
        prods.append((sc, q_c, q_n))

    hidden = []
    for sh, (bi, h) in enumerate(chains):
        sc, q_c, q_n = prods[sh]
        w_inter = gate[sh]['w_inter']
        num = jnp.dot(sc.astype(BF16), v_ref[bi, :, hsl(h)], preferred_element_type=F32) + w_inter * q_c
        den = jnp.sum(sc, axis=-1, keepdims=True) + w_inter * q_n
        den = jnp.maximum(jnp.abs(den), gate[sh]['emt'])
        hidden.append(num / jnp.concatenate([den, den], axis=-1))

    for sh, (bi, h) in enumerate(chains):
        hs = hsl(h)
        hg = _sigmoid(op_ref[bi, :, hs].astype(F32)) * hidden[sh]
        mu = jnp.mean(hg, axis=-1, keepdims=True)
        hc = hg - mu
        var = jnp.mean(hc * hc, axis=-1, keepdims=True)
        hn = hc * lax.rsqrt(var + EPS) * hng_ref[:, hs]
        o_ref[bi, :, hs] = (hn + skip_ref[:, hs] * xc_ref[bi, :, hs].astype(F32)).astype(BF16)

    for sh, (bi, h) in enumerate(chains):
        decay = gate[sh]['decay']
        wkt = kt_ref[bi, hsl(h), :].astype(F32) * gate[sh]['wrow']
        c_scr[sh] = decay * c_scr[sh] + jnp.dot(wkt.astype(BF16), v_ref[bi, :, hsl(h)],
                                                preferred_element_type=F32)
        n_scr[sh] = decay * n_scr[sh] + jnp.sum(wkt, axis=-1, keepdims=True)
        m_scr[sh] = jnp.broadcast_to(gate[sh]['m_new'], (1, LANES))


def _mlstm(q, kt, v, bc, g, o_pre, xc, hn_g, skip):
    b, s, d = q.shape
    nc = s // CHUNK
    nb = NB_MLSTM if b % NB_MLSTM == 0 else 1
    tok = pl.BlockSpec((nb, CHUNK, d), lambda bi, ci: (bi, ci, 0))
    vec = pl.BlockSpec((1, d), lambda bi, ci: (0, 0))
    return pl.pallas_call(
        _mlstm_kernel,
        out_shape=jax.ShapeDtypeStruct((b, s, d), BF16),
        grid=(b // nb, nc),
        in_specs=[tok,
                  pl.BlockSpec((nb, d, CHUNK), lambda bi, ci: (bi, 0, ci)),
                  tok,
                  pl.BlockSpec((nb, CHUNK, LANES), lambda bi, ci: (bi, ci, 0)),
                  pl.BlockSpec((nb, 1, 8, CHUNK), lambda bi, ci: (bi, ci, 0, 0)),
                  tok, tok, vec, vec],
        out_specs=tok,
        scratch_shapes=[pltpu.VMEM((nb * N_HEADS, HEAD_DIM, HEAD_DIM), F32),
                        pltpu.VMEM((nb * N_HEADS, HEAD_DIM, LANES), F32),
                        pltpu.VMEM((nb * N_HEADS, 1, LANES), F32)],
        compiler_params=_cparams(("arbitrary", "arbitrary")),
        name="mlstm",
    )(q, kt, v, bc, g, o_pre, xc, hn_g, skip)


def _post_kernel(x_ref, aa_ref, mo_ref, za_ref, zb_ref, mod_ref, wpw_ref, bpw_ref, wmo_ref, wout_ref,
                 gmoe_ref, wrh_ref, wrl_ref, br_ref,
                 x1_ref, hm_ref, mi_ref, mw_ref, cnt_ref, carry):
    first = (pl.program_id(0) == 0) & (pl.program_id(1) == 0)

    @pl.when(first)
    def _():
        carry[...] = jnp.zeros_like(carry)

    tm = x_ref.shape[1]
    y_a = jnp.dot(aa_ref[0], wpw_ref[...], preferred_element_type=F32) + bpw_ref[...]
    y_b = jnp.dot(mo_ref[0], wmo_ref[...], preferred_element_type=F32)
    merged = _sigmoid(za_ref[0].astype(F32)) * y_a + _sigmoid(zb_ref[0].astype(F32)) * y_b
    mixed = jnp.dot(merged.astype(BF16), wout_ref[...], preferred_element_type=F32)
    x1 = x_ref[0] + mod_ref[0, 2:3, :] * mixed
    x1_ref[0] = x1
    hm = _ada_norm(x1, gmoe_ref[...], mod_ref[0, 3:4, :], mod_ref[0, 4:5, :])
    hm_ref[0] = hm

    hm_hi = hm.astype(BF16)
    hm_lo = (hm - hm_hi.astype(F32)).astype(BF16)
    logits = (jnp.dot(hm_hi, wrh_ref[...], preferred_element_type=F32)
              + (jnp.dot(hm_lo, wrh_ref[...], preferred_element_type=F32)
                 + jnp.dot(hm_hi, wrl_ref[...], preferred_element_type=F32))) + br_ref[...]
    lane = lax.broadcasted_iota(jnp.int32, (tm, LANES), 1)
    work = jnp.where(lane < N_EXPERTS, logits, -jnp.inf)
    vals, idxs = [], []
    for _ in range(TOP_K):
        mx = jnp.max(work, axis=-1, keepdims=True)
        ix = jnp.min(jnp.where(work == mx, lane, LANES), axis=-1, keepdims=True)
        vals.append(mx)
        idxs.append(ix)
        work = jnp.where(lane == ix, -jnp.inf, work)
    exps = [jnp.exp(v - vals[0]) for v in vals]
    tot = exps[0] + exps[1] + exps[2] + exps[3]

    onehots = [lane == ix for ix in idxs]
    occ = jnp.zeros((tm, LANES), F32)
    for oh in onehots:
        occ = occ + oh.astype(F32)
    r_i = lax.broadcasted_iota(jnp.int32, (tm, tm), 0)
    c_i = lax.broadcasted_iota(jnp.int32, (tm, tm), 1)
    strict = (c_i < r_i).astype(BF16)
    before = jnp.dot(strict, occ.astype(BF16), preferred_element_type=F32) + carry[...]
    mi = jnp.zeros((tm, LANES), jnp.int32)
    mw = jnp.zeros((tm, LANES), F32)
    for j in range(TOP_K):
        rank = jnp.sum(jnp.where(onehots[j], before, 0.0), axis=-1, keepdims=True).astype(jnp.int32)
        mi = jnp.where(lane == j, idxs[j], mi)
        mi = jnp.where(lane == TOP_K + j, rank, mi)
        mw = jnp.where(lane == j, exps[j] / tot, mw)
    mi_ref[0] = mi
    mw_ref[0] = mw
    total = carry[...] + jnp.sum(occ, axis=0, keepdims=True)
    carry[...] = total
    cnt_ref[...] = jnp.broadcast_to(total, cnt_ref.shape)


def _post(x, act_a, mo_in, z_a, z_b, mod, w_pw, b_pw, w_mo, w_out, g_moe, w_r_hi, w_r_lo, b_r):
    b, s, d = x.shape
    tm = min(TM_POST, s)
    tok = pl.BlockSpec((1, tm, d), lambda bi, i: (bi, i, 0))
    meta = pl.BlockSpec((1, tm, LANES), lambda bi, i: (bi, i, 0))
    vec = pl.BlockSpec((1, d), lambda bi, i: (0, 0))
    mat = pl.BlockSpec((d, d), lambda bi, i: (0, 0))
    wr = pl.BlockSpec((d, LANES), lambda bi, i: (0, 0))
    return pl.pallas_call(
        _post_kernel,
        out_shape=[jax.ShapeDtypeStruct((b, s, d), F32),
                   jax.ShapeDtypeStruct((b, s, d), F32),
                   jax.ShapeDtypeStruct((b, s, LANES), jnp.int32),
                   jax.ShapeDtypeStruct((b, s, LANES), F32),
                   jax.ShapeDtypeStruct((8, LANES), F32)],
        grid=(b, s // tm),
        in_specs=[tok, tok, tok, tok, tok,
                  pl.BlockSpec((1, 6, d), lambda bi, i: (bi, 0, 0)),
                  mat, vec, mat, mat, vec, wr, wr,
                  pl.BlockSpec((1, LANES), lambda bi, i: (0, 0))],
        out_specs=[tok, tok, meta, meta, pl.BlockSpec((8, LANES), lambda bi, i: (0, 0))],
        scratch_shapes=[pltpu.VMEM((1, LANES), F32)],
        compiler_params=_cparams(("arbitrary", "arbitrary")),
        name="post",
    )(x, act_a, mo_in, z_a, z_b, mod, w_pw, b_pw, w_mo, w_out, g_moe, w_r_hi, w_r_lo, b_r)


def _dispatch_kernel(zlist_ref, dest_ref, hm_ref, xs_hbm, zbuf, sem, zsem):
    i = pl.program_id(0)

    @pl.when(i == 0)
    def _():
        zbuf[...] = jnp.zeros_like(zbuf)

        def zero_copy(blk):
            return pltpu.make_async_copy(zbuf, xs_hbm.at[pl.ds(blk * BM_EXPERT, BM_EXPERT)], zsem)

        def start(k, carry):
            @pl.when(zlist_ref[k] >= 0)
            def _():
                zero_copy(zlist_ref[k]).start()
            return carry

        def wait(k, carry):
            @pl.when(zlist_ref[k] >= 0)
            def _():
                zero_copy(zlist_ref[k]).wait()
            return carry

        lax.fori_loop(0, zlist_ref.shape[0], start, 0)
        lax.fori_loop(0, zlist_ref.shape[0], wait, 0)

    groups = hm_ref.shape[0]
    rows = groups * 8

    def body(r8, carry):
        base = r8 * (8 * TOP_K)
        for k in range(8):
            for j in range(TOP_K):
                dst = dest_ref[base + k * TOP_K + j]
                pltpu.make_async_copy(hm_ref.at[r8, pl.ds(k, 1)], xs_hbm.at[pl.ds(dst, 1)],
                                      sem).start(priority=(k * TOP_K + j) % 2)
        return carry

    lax.fori_loop(0, groups, body, 0)
    for j in range(TOP_K):
        pltpu.make_async_copy(xs_hbm.at[pl.ds(0, rows)], xs_hbm.at[pl.ds(rows, rows)], sem).wait()


def _dispatch(zlist, dest_flat, hm_flat, n_rows):
    t, d = hm_flat.shape
    tr = min(TR_DISPATCH, t)
    blk = TOP_K * tr
    grid_spec = pltpu.PrefetchScalarGridSpec(
        num_scalar_prefetch=1,
        grid=(t // tr,),
        in_specs=[pl.BlockSpec((blk,), lambda i, zl: (i,), memory_space=pltpu.SMEM),
                  pl.BlockSpec((tr // 8, 8, d), lambda i, zl: (i, 0, 0))],
        out_specs=pl.BlockSpec(memory_space=pl.ANY),
        scratch_shapes=[pltpu.VMEM((BM_EXPERT, d), F32),
                        pltpu.SemaphoreType.DMA(()), pltpu.SemaphoreType.DMA(())],
    )
    return pl.pallas_call(
        _dispatch_kernel,
        out_shape=jax.ShapeDtypeStruct((n_rows, d), F32),
        grid_spec=grid_spec,
        compiler_params=_cparams(("arbitrary",)),
        name="dispatch",
    )(zlist, dest_flat, hm_flat.reshape(t // 8, 8, d))


def _expert_kernel(be_ref, nu_ref, xs_ref, wgu_ref, bgu_ref, wdn_ref, bdn_ref, ys_ref, wgu_bf, wdn_bf):
    i = pl.program_id(0)
    d = D_MODEL
    prev = be_ref[jnp.maximum(i - 1, 0)]
    changed = (i == 0) | (be_ref[i] != prev)

    @pl.when(changed)
    def _():
        wgu_bf[...] = wgu_ref[0].astype(BF16)
        wdn_bf[...] = wdn_ref[0].astype(BF16)

    @pl.when(i < nu_ref[0])
    def _():
        x = xs_ref[...].astype(BF16)
        acc = jnp.zeros(ys_ref.shape, F32) + bdn_ref[0]
        ck = 512
        for n in range(d // ck):
            cg = slice(n * ck, (n + 1) * ck)
            cu = slice(d + n * ck, d + (n + 1) * ck)
            gate = jnp.dot(x, wgu_bf[:, cg], preferred_element_type=F32) + bgu_ref[0, :, cg]
            up = jnp.dot(x, wgu_bf[:, cu], preferred_element_type=F32) + bgu_ref[0, :, cu]
            gate = jnp.minimum(gate, SWIGLU_LIMIT)
            up = jnp.clip(up, -SWIGLU_LIMIT, SWIGLU_LIMIT)
            glu = gate * _sigmoid(gate * SWIGLU_ALPHA)
            act = ((up + 1.0) * glu).astype(BF16)
            acc = acc + jnp.dot(act, wdn_bf[cg, :], preferred_element_type=F32)
        ys_ref[...] = acc

    @pl.when(i >= nu_ref[0])
    def _():
        ys_ref[...] = jnp.zeros_like(ys_ref)


def _experts(blk_e, n_used, xs, w_gu, b_gu, w_dn, b_dn):
    p, d = xs.shape
    bm = BM_EXPERT
    e = w_gu.shape[0]
    grid_spec = pltpu.PrefetchScalarGridSpec(
        num_scalar_prefetch=2,
        grid=(p // bm,),
        in_specs=[pl.BlockSpec((bm, d), lambda i, be, nu: (i, 0)),
                  pl.BlockSpec((1, d, 2 * d), lambda i, be, nu: (be[i], 0, 0)),
                  pl.BlockSpec((1, 1, 2 * d), lambda i, be, nu: (be[i], 0, 0)),
                  pl.BlockSpec((1, d, d), lambda i, be, nu: (be[i], 0, 0)),
                  pl.BlockSpec((1, 1, d), lambda i, be, nu: (be[i], 0, 0))],
        out_specs=pl.BlockSpec((bm, d), lambda i, be, nu: (i, 0)),
        scratch_shapes=[pltpu.VMEM((d, 2 * d), BF16), pltpu.VMEM((d, d), BF16)],
    )
    return pl.pallas_call(
        _expert_kernel,
        out_shape=jax.ShapeDtypeStruct((p, d), F32),
        grid_spec=grid_spec,
        compiler_params=_cparams(("arbitrary",)),
        name="experts",
    )(blk_e, n_used, xs, w_gu, b_gu.reshape(e, 1, 2 * d), w_dn, b_dn.reshape(e, 1, d))


def _combine_kernel(dcur_ref, dnext_ref, ys_hbm, x1_ref, mw_ref, mod_ref, gf_ref, o_ref, gbuf, sem):
    i = pl.program_id(0)
    n = pl.num_programs(0)
    slot = i % 2

    def issue(dref, sl):
        def body(r8, carry):
            base = r8 * (8 * TOP_K)
            for k in range(8):
                for j in range(TOP_K):
                    src = dref[base + k * TOP_K + j]
                    pltpu.make_async_copy(ys_hbm.at[pl.ds(src, 1)], gbuf.at[sl, j, r8, pl.ds(k, 1)],
                                          sem.at[sl]).start(priority=(k * TOP_K + j) % 2)
            return carry
        lax.fori_loop(0, TR_ROWS // 8, body, 0)

    @pl.when(i == 0)
    def _():
        issue(dcur_ref, 0)

    @pl.when(i + 1 < n)
    def _():
        issue(dnext_ref, 1 - slot)

    for j in range(TOP_K):
        pltpu.make_async_copy(gbuf.at[1 - slot, j], gbuf.at[slot, j], sem.at[slot]).wait()

    moe = jnp.zeros((TR_ROWS, D_MODEL), F32)
    for j in range(TOP_K):
        moe = moe + mw_ref[:, j:j + 1] * gbuf[slot, j].reshape(TR_ROWS, D_MODEL)
    xo = x1_ref[...] + mod_ref[0, 5:6, :] * moe
    ms = jnp.mean(xo * xo, axis=-1, keepdims=True)
    o_ref[...] = xo * lax.rsqrt(ms + EPS) * gf_ref[...]


def _combine(dest_flat, ys, x1_flat, mw_flat, mod, g_final, seq):
    t, d = x1_flat.shape
    blk = TOP_K * TR_ROWS
    nsteps = t // TR_ROWS
    per_seq = seq // TR_ROWS
    return pl.pallas_call(
        _combine_kernel,
        out_shape=jax.ShapeDtypeStruct((t, d), F32),
        grid=(nsteps,),
        in_specs=[pl.BlockSpec((blk,), lambda i: (i,), memory_space=pltpu.SMEM),
                  pl.BlockSpec((blk,), lambda i: (jnp.minimum(i + 1, nsteps - 1),), memory_space=pltpu.SMEM),
                  pl.BlockSpec(memory_space=pl.ANY),
                  pl.BlockSpec((TR_ROWS, d), lambda i: (i, 0)),
                  pl.BlockSpec((TR_ROWS, LANES), lambda i: (i, 0)),
                  pl.BlockSpec((1, 6, d), lambda i: (i // per_seq, 0, 0)),
                  pl.BlockSpec((1, d), lambda i: (0, 0))],
        out_specs=pl.BlockSpec((TR_ROWS, d), lambda i: (i, 0)),
        scratch_shapes=[pltpu.VMEM((2, TOP_K, TR_ROWS // 8, 8, d), F32), pltpu.SemaphoreType.DMA((2,))],
        compiler_params=_cparams(("arbitrary",)),
        name="combine",
    )(dest_flat, dest_flat, ys, x1_flat, mw_flat, mod, g_final)


def _blockdiag_tiles(w):
    nb = w.shape[0]
    per = LANES // 4
    wt = w.reshape(nb // per, per, 4, 4)
    eye = jnp.eye(per, dtype=w.dtype)
    return jnp.einsum('cnio,nm->cnimo', wt, eye).reshape(nb // per, LANES, LANES)


def _gate_weights(w_if_part):
    out = jnp.zeros((w_if_part.shape[0], 2 * LANES), F32)
    out = out.at[:, :N_HEADS].set(w_if_part[:, :N_HEADS])
    out = out.at[:, LANES:LANES + N_HEADS].set(w_if_part[:, N_HEADS:])
    return out.astype(BF16)


def _layer(x, mod, p):
    b, s, d = x.shape
    t = b * s
    row = lambda v: v.reshape(1, -1)

    glu, xm, o_pre, z_a, z_b = _inproj(x, mod, row(p['g_mix']), p['w_in'].astype(BF16))
    act_a = _conv_branch(glu, p['w_dw'], row(p['b_dw']), row(p['ln_g']), row(p['ln_b']))

    bdq = _blockdiag_tiles(p['w_qbd']).astype(BF16)
    bdk = _blockdiag_tiles(p['w_kbd']).astype(BF16)
    bdv = _blockdiag_tiles(p['w_vbd']).astype(BF16)
    bdkt = jnp.swapaxes(bdk, 1, 2)
    w_if = p['w_if']
    bif = jnp.zeros((1, 2 * LANES), F32)
    bif = bif.at[0, :N_HEADS].set(p['b_if'][:N_HEADS]).at[0, LANES:LANES + N_HEADS].set(p['b_if'][N_HEADS:])
    xc, q, kt, v, bc, g = _mlstm_pre(xm, p['w_sc'], row(p['b_sc']), bdq, bdk, bdkt, bdv,
                                     _gate_weights(w_if[:d]), _gate_weights(w_if[d:2 * d]),
                                     _gate_weights(w_if[2 * d:]), bif)
    mo_in = _mlstm(q, kt, v, bc, g, o_pre, xc, row(p['hn_g']), row(p['skip']))

    w_r = jnp.zeros((d, LANES), F32).at[:, :N_EXPERTS].set(p['w_router'])
    b_r = jnp.zeros((1, LANES), F32).at[0, :N_EXPERTS].set(p['b_router'])
    w_r_hi = w_r.astype(BF16)
    w_r_lo = (w_r - w_r_hi.astype(F32)).astype(BF16)
    x1, hm, mi, mw, cnt = _post(x, act_a, mo_in, z_a, z_b, mod,
                                p['w_conv_out'].astype(BF16), row(p['b_conv_out']),
                                p['w_m_out'].astype(BF16), p['w_out'].astype(BF16),
                                row(p['g_moe']), w_r_hi, w_r_lo, b_r)

    bm = BM_EXPERT
    counts = cnt[0, :N_EXPERTS].astype(jnp.int32)
    padded = (counts + bm - 1) // bm * bm
    pad_end = jnp.cumsum(padded)
    start_p = pad_end - padded
    n_rows = t * TOP_K + N_EXPERTS * bm
    nblk = n_rows // bm
    blk_start = jnp.arange(nblk, dtype=jnp.int32) * bm
    blk_e = jnp.minimum(jnp.sum((pad_end[None, :] <= blk_start[:, None]).astype(jnp.int32), axis=1),
                        N_EXPERTS - 1)
    n_used = (pad_end[-1:] // bm).astype(jnp.int32)
    mi = mi.reshape(t, LANES)
    dest = (start_p[mi[:, :TOP_K]] + mi[:, TOP_K:2 * TOP_K]).reshape(t * TOP_K)

    last_blk = jnp.where(padded > 0, pad_end // bm - 1, -1)
    tail_blk = n_used[0] + jnp.arange(N_EXPERTS, dtype=jnp.int32)
    zlist = jnp.concatenate([last_blk, jnp.where(tail_blk < nblk, tail_blk, -1)]).astype(jnp.int32)

    xs = _dispatch(zlist, dest, hm.reshape(t, d), n_rows)
    ys = _experts(blk_e, n_used, xs, p['w_gu'], p['b_gu'], p['w_down'], p['b_down'])
    out = _combine(dest, ys, x1.reshape(t, d), mw.reshape(t, LANES), mod, p['g_final'], s)
    return out.reshape(b, s, d)


def kernel(x, c, w_ada, b_ada, g_mix, w_in, w_dw, b_dw, ln_g, ln_b, w_conv_out, b_conv_out, w_sc, b_sc,
           w_qbd, w_kbd, w_vbd, w_if, b_if, hn_g, skip, w_m_out, w_out, g_moe, w_router, b_router,
           w_gu, b_gu, w_down, b_down, g_final):
    b, s, d = x.shape
    depth = w_ada.shape[0]
    assert depth == 1 and d == D_MODEL and s % 512 == 0
    rows = -(-b // 8) * 8
    c_pad = jnp.zeros((rows, d), F32).at[:b].set(c)
    l = 0
    mod = _ada(c_pad, w_ada[l], b_ada[l].reshape(1, -1))[:b].reshape(b, 6, d)
    p = dict(g_mix=g_mix[l], w_in=w_in[l], w_dw=w_dw[l], b_dw=b_dw[l], ln_g=ln_g[l], ln_b=ln_b[l],
             w_conv_out=w_conv_out[l], b_conv_out=b_conv_out[l], w_sc=w_sc[l], b_sc=b_sc[l],
             w_qbd=w_qbd[l], w_kbd=w_kbd[l], w_vbd=w_vbd[l], w_if=w_if[l], b_if=b_if[l],
             hn_g=hn_g[l], skip=skip[l], w_m_out=w_m_out[l], w_out=w_out[l], g_moe=g_moe[l],
             w_router=w_router[l], b_router=b_router[l], w_gu=w_gu[l], b_gu=b_gu[l],
             w_down=w_down[l], b_down=b_down[l], g_final=g_final.reshape(1, -1))
    return _layer(x, mod, p)
```
